```python
import math
import jax, jax.numpy as jnp
from jax import lax
import numpy as np

D_MODEL = 4096
BATCH = 2
SEQ = 4096
DEPTH = 2

CTX_LEN = 256
GRID_W = 64

MLA_HEADS = 16
MLA_Q_RANK = 1024
MLA_KV_RANK = 512
MLA_NOPE = 128
MLA_ROPE = 64
MLA_V = 128
MLA_QK = MLA_NOPE + MLA_ROPE
ROPE_THETA = 10000.0
Q_BLOCK = 128

SC_WIDTH = 2048
SC_CONV_W = 3

GDN_HEADS = 16
GDN_DK = 128
GDN_DV = 128
GDN_QK = GDN_HEADS * GDN_DK
GDN_VW = GDN_HEADS * GDN_DV
GDN_CONV_W = 3
GDN_CHUNK = 64
N_DIR = 2

BRANCH_W = 2048
N_BRANCH = 3
ADA_MULT = 6

N_EXPERTS = 16
N_GROUPS = 4
EXPERTS_PER_GROUP = N_EXPERTS // N_GROUPS
GROUP_SCORE_TOPK = 2
TOP_GROUPS = 1
TOP_K = 2
D_EXPERT = 1024

DEEPNORM_ALPHA = (2 * DEPTH) ** 0.25
DEEPNORM_BETA = (8 * DEPTH) ** -0.25
EPS = 1e-6

IN_LAYOUT = (
    ("mla_cq", MLA_Q_RANK), ("mla_ckv", MLA_KV_RANK), ("mla_krope", MLA_ROPE),
    ("sc_b", SC_WIDTH), ("sc_c", SC_WIDTH), ("sc_x", SC_WIDTH),
    ("gdn_q", GDN_QK), ("gdn_k", GDN_QK), ("gdn_v", GDN_VW), ("gdn_z", GDN_VW),
    ("gdn_a", N_DIR * GDN_HEADS), ("gdn_beta", N_DIR * GDN_HEADS),
    ("gates", N_BRANCH * D_MODEL),
)
IN_COLS = sum(w for _, w in IN_LAYOUT)
CTX_STATE_COLS = ("mla_ckv", "mla_krope", "gdn_k", "gdn_v", "gdn_a", "gdn_beta")

kernel_name = "hybrid_mla_conv_gdn_moe_prefix_trunk"


def rms_norm(x, w):
    xf = x.astype(jnp.float32)
    y = xf * lax.rsqrt(jnp.mean(xf * xf, axis=-1, keepdims=True) + EPS)
    return (y * w.astype(jnp.float32)).astype(x.dtype)


def layer_norm(x, g, b):
    xf = x.astype(jnp.float32)
    xc = xf - jnp.mean(xf, axis=-1, keepdims=True)
    var = jnp.mean(xc * xc, axis=-1, keepdims=True)
    return (xc * lax.rsqrt(var + EPS) * g.astype(jnp.float32) + b.astype(jnp.float32)).astype(x.dtype)


def l2_norm(x):
    xf = x.astype(jnp.float32)
    return xf * lax.rsqrt(jnp.sum(xf * xf, axis=-1, keepdims=True) + EPS)


def modulate(x, shift, scale):
    return x * (1 + scale) + shift


def axial_rope_tables(n_tok):
    rows = n_tok // GRID_W
    row = jnp.repeat(jnp.arange(rows), GRID_W).astype(jnp.float32)
    col = (jnp.arange(rows * GRID_W) % GRID_W).astype(jnp.float32)
    n_freq = MLA_ROPE // 4
    inv = ROPE_THETA ** (-jnp.arange(n_freq, dtype=jnp.float32) / n_freq)
    ang = jnp.stack([row[:, None] * inv, col[:, None] * inv], axis=1)
    return jnp.cos(ang), jnp.sin(ang)


def apply_axial_rope(x, cos, sin):
    shp = x.shape
    xr = x.reshape(shp[:-1] + (2, 2, shp[-1] // 4))
    a, b = xr[..., 0, :], xr[..., 1, :]
    cos = cos.astype(x.dtype)
    sin = sin.astype(x.dtype)
    return jnp.stack([a * cos - b * sin, a * sin + b * cos], axis=-2).reshape(shp)


def dwconv_centred(x, w):
    pad = w.shape[0] // 2
    return lax.conv_general_dilated(
        x, w[:, None, :].astype(x.dtype), window_strides=(1,), padding=[(pad, pad)],
        dimension_numbers=("NWC", "WIO", "NWC"), feature_group_count=x.shape[-1])


def in_project(h, w_in, names=None):
    offs, o = {}, 0
    for n, width in IN_LAYOUT:
        offs[n] = (o, width)
        o += width
    if names is None:
        p = h @ w_in
        return {n: p[..., s:s + wd] for n, (s, wd) in offs.items()}
    return {n: h @ w_in[:, offs[n][0]:offs[n][0] + offs[n][1]] for n in names}


def mla_queries(cq, q_norm, w_uq, rope):
    b, t, _ = cq.shape
    q = (rms_norm(cq, q_norm) @ w_uq).reshape(b, t, MLA_HEADS, MLA_QK)
    q_nope, q_rope = q[..., :MLA_NOPE], q[..., MLA_NOPE:]
    if rope is not None:
        cos, sin = rope
        q_rope = apply_axial_rope(q_rope, cos[None, :, None], sin[None, :, None])
    return jnp.concatenate([q_nope, q_rope], axis=-1) * (MLA_QK ** -0.5)


def mla_keys_values(ckv, k_rope, kv_norm, w_ukv, rope):
    b, t, _ = ckv.shape
    kv = (rms_norm(ckv, kv_norm) @ w_ukv).reshape(b, t, MLA_HEADS, MLA_NOPE + MLA_V)
    k_nope, v = kv[..., :MLA_NOPE], kv[..., MLA_NOPE:]
    if rope is not None:
        cos, sin = rope
        k_rope = apply_axial_rope(k_rope, cos[None], sin[None])
    k_rope = jnp.broadcast_to(k_rope[:, :, None, :], (b, t, MLA_HEADS, MLA_ROPE))
    return jnp.concatenate([k_nope, k_rope], axis=-1), v


def softmax_attend(q, k, v):
    s = jnp.einsum("bqhd,bkhd->bhqk", q, k, preferred_element_type=jnp.float32)
    p = jax.nn.softmax(s, axis=-1).astype(v.dtype)
    return jnp.einsum("bhqk,bkhd->bqhd", p, v)


def latent_attention(q, k_lat, v_lat, k_ctx, v_ctx):
    b, t, h, d = q.shape
    nb = t // Q_BLOCK
    k_all = jnp.concatenate([k_ctx, k_lat], axis=1)
    v_all = jnp.concatenate([v_ctx, v_lat], axis=1)
    q_blocks = jnp.moveaxis(q.reshape(b, nb, Q_BLOCK, h, d), 1, 0)
    o = lax.map(lambda qb: softmax_attend(qb, k_all, v_all), q_blocks)
    return jnp.moveaxis(o, 0, 1).reshape(b, t, h * v_lat.shape[-1])


def short_conv_mix(p, sc_conv):
    return p["sc_b"] * dwconv_centred(p["sc_c"] * p["sc_x"], sc_conv)


def gdn_features(p, conv_w, a_log, dt_bias, with_q):
    names = ("gdn_q", "gdn_k", "gdn_v") if with_q else ("gdn_k", "gdn_v")
    w = conv_w if with_q else conv_w[:, GDN_QK:]
    qkv = jax.nn.silu(dwconv_centred(jnp.concatenate([p[n] for n in names], axis=-1), w)).astype(jnp.float32)
    b, t, _ = qkv.shape
    q = None
    if with_q:
        q = l2_norm(qkv[..., :GDN_QK].reshape(b, t, GDN_HEADS, GDN_DK)) * (GDN_DK ** -0.5)
        qkv = qkv[..., GDN_QK:]
    k = l2_norm(qkv[..., :GDN_QK].reshape(b, t, GDN_HEADS, GDN_DK))
    v = qkv[..., GDN_QK:].reshape(b, t, GDN_HEADS, GDN_DV)
    a = p["gdn_a"].astype(jnp.float32).reshape(b, t, N_DIR, GDN_HEADS)
    g = -jnp.exp(a_log.astype(jnp.float32)) * jax.nn.softplus(a + dt_bias.astype(jnp.float32))
    beta = jax.nn.sigmoid(p["gdn_beta"].astype(jnp.float32).reshape(b, t, N_DIR, GDN_HEADS))
    return q, k, v, g, beta


def gdn_chunk_scan(q, k, v, g, beta, s0):
    b, t, h, _ = k.shape
    n = t // GDN_CHUNK

    def to_chunks(a):
        a = a.reshape((b, n, GDN_CHUNK, h) + a.shape[3:])
        return jnp.moveaxis(a, (1, 3), (0, 2))

    k, v, g, beta = to_chunks(k), to_chunks(v), to_chunks(g), to_chunks(beta)
    gcum = jnp.cumsum(g, axis=-1)
    idx = jnp.arange(GDN_CHUNK)
    lower = idx[:, None] >= idx[None, :]
    strict = idx[:, None] > idx[None, :]
    decay = jnp.exp(jnp.where(lower, gcum[..., :, None] - gcum[..., None, :], -jnp.inf))
    k_beta = k * beta[..., None]
    a_strict = jnp.where(strict, jnp.einsum("...ik,...jk->...ij", k_beta, k) * decay, 0.0)
    eye = jnp.eye(GDN_CHUNK, dtype=jnp.float32)
    t_inv = lax.linalg.triangular_solve(eye + a_strict, jnp.broadcast_to(eye, a_strict.shape),
                                        left_side=True, lower=True, unit_diagonal=True)
    u = t_inv @ (v * beta[..., None])
    w = t_inv @ (k_beta * jnp.exp(gcum)[..., None])
    k_tail = k * jnp.exp(gcum[..., -1:] - gcum)[..., None]
    chunk_decay = jnp.exp(gcum[..., -1])
    with_out = q is not None
    xs = (u, w, k_tail, chunk_decay)
    if with_out:
        q = to_chunks(q)
        intra = jnp.einsum("...ik,...jk->...ij", q, k) * decay
        xs = xs + (q * jnp.exp(gcum)[..., None], intra)

    def step(state, inp):
        u_i, w_i, kt_i, cd_i = inp[:4]
        v_new = u_i - jnp.einsum("bhck,bhkv->bhcv", w_i, state)
        new_state = state * cd_i[..., None, None] + jnp.einsum("bhck,bhcv->bhkv", kt_i, v_new)
        if not with_out:
            return new_state, None
        qd_i, intra_i = inp[4:]
        o_i = jnp.einsum("bhck,bhkv->bhcv", qd_i, state) + jnp.einsum("bhcj,bhjv->bhcv", intra_i, v_new)
        return new_state, o_i

    s_fin, o = lax.scan(step, s0, xs)
    if not with_out:
        return None, s_fin
    o = jnp.moveaxis(o, (0, 2), (1, 3)).reshape(b, t, h, v.shape[-1])
    return o, s_fin


def gdn_bidirectional(feat_lat, feat_ctx, with_ctx):
    ql, kl, vl, gl, bl = feat_lat
    qc, kc, vc, gc, bc = feat_ctx
    bsz = kl.shape[0]
    lat_outs, ctx_outs = [], []
    for d in range(N_DIR):
        rev = (lambda a: a[:, ::-1]) if d == 1 else (lambda a: a)
        s0 = jnp.zeros((bsz, GDN_HEADS, GDN_DK, GDN_DV), jnp.float32)
        oc, s_ctx = gdn_chunk_scan(rev(qc) if with_ctx else None, rev(kc), rev(vc),
                                   rev(gc[:, :, d]), rev(bc[:, :, d]), s0)
        ol, _ = gdn_chunk_scan(rev(ql), rev(kl), rev(vl), rev(gl[:, :, d]), rev(bl[:, :, d]), s_ctx)
        lat_outs.append(rev(ol))
        if with_ctx:
            ctx_outs.append(rev(oc))
    return lat_outs[0] + lat_outs[1], (ctx_outs[0] + ctx_outs[1] if with_ctx else None)


def gdn_output(o, z, gdn_norm):
    b, t = o.shape[:2]
    y = rms_norm(o, gdn_norm) * jax.nn.silu(z.reshape(b, t, GDN_HEADS, GDN_DV).astype(jnp.float32))
    return y.reshape(b, t, GDN_VW).astype(z.dtype)


def merge_branches(branches, gate_logits, w_branch, w_o):
    br = jnp.stack(branches, axis=-2)
    proj = jnp.einsum("btnw,nwd->btnd", br, w_branch)
    gates = jax.nn.sigmoid(gate_logits.reshape(proj.shape))
    return jnp.sum(gates * proj, axis=-2) @ w_o


def token_mixer(h_lat, h_ctx, rope, w_in, q_norm, kv_norm, w_uq, w_ukv, sc_conv, gdn_conv,
                a_log, dt_bias, gdn_norm, w_branch, w_o, with_ctx):
    p_lat = in_project(h_lat, w_in)
    p_ctx = in_project(h_ctx, w_in) if with_ctx else in_project(h_ctx, w_in, CTX_STATE_COLS)
    k_lat, v_lat = mla_keys_values(p_lat["mla_ckv"], p_lat["mla_krope"], kv_norm, w_ukv, rope)
    k_ctx, v_ctx = mla_keys_values(p_ctx["mla_ckv"], p_ctx["mla_krope"], kv_norm, w_ukv, None)
    q_lat = mla_queries(p_lat["mla_cq"], q_norm, w_uq, rope)
    attn_lat = latent_attention(q_lat, k_lat, v_lat, k_ctx, v_ctx)
    conv_lat = short_conv_mix(p_lat, sc_conv)
    feat_lat = gdn_features(p_lat, gdn_conv, a_log, dt_bias, True)
    feat_ctx = gdn_features(p_ctx, gdn_conv, a_log, dt_bias, with_ctx)
    o_lat, o_ctx = gdn_bidirectional(feat_lat, feat_ctx, with_ctx)
    gdn_lat = gdn_output(o_lat, p_lat["gdn_z"], gdn_norm)
    out_lat = merge_branches((attn_lat, conv_lat, gdn_lat), p_lat["gates"], w_branch, w_o)
    if not with_ctx:
        return out_lat, None
    b, l, _ = h_ctx.shape
    q_ctx = mla_queries(p_ctx["mla_cq"], q_norm, w_uq, None)
    attn_ctx = softmax_attend(q_ctx, k_ctx, v_ctx).reshape(b, l, MLA_HEADS * MLA_V)
    conv_ctx = short_conv_mix(p_ctx, sc_conv)
    gdn_ctx = gdn_output(o_ctx, p_ctx["gdn_z"], gdn_norm)
    out_ctx = merge_branches((attn_ctx, conv_ctx, gdn_ctx), p_ctx["gates"], w_branch, w_o)
    return out_lat, out_ctx


def routed_moe(h, w_router, router_bias, w_gate, w_up, w_down):
    t = h.shape[0]
    aff = jax.nn.sigmoid(jnp.einsum("td,de->te", h, w_router, preferred_element_type=jnp.float32))
    sel = aff + router_bias.astype(jnp.float32)
    grp_score = lax.top_k(sel.reshape(t, N_GROUPS, EXPERTS_PER_GROUP), GROUP_SCORE_TOPK)[0].sum(-1)
    _, g_idx = lax.top_k(grp_score, TOP_GROUPS)
    g_mask = jnp.any(g_idx[:, :, None] == jnp.arange(N_GROUPS), axis=1)
    e_mask = jnp.repeat(g_mask, EXPERTS_PER_GROUP, axis=-1)
    _, e_idx = lax.top_k(jnp.where(e_mask, sel, -jnp.inf), TOP_K)
    w_sel = jnp.take_along_axis(aff, e_idx, axis=-1)
    w_sel = w_sel / jnp.sum(w_sel, axis=-1, keepdims=True)
    combine = jnp.einsum("tk,tke->te", w_sel, jax.nn.one_hot(e_idx, N_EXPERTS, dtype=jnp.float32))
    gate = jnp.einsum("td,edf->tef", h, w_gate)
    up = jnp.einsum("td,edf->tef", h, w_up)
    act = jax.nn.silu(gate) * up * combine[:, :, None].astype(h.dtype)
    return jnp.einsum("tef,efd->td", act, w_down)


def setup_inputs(seed: int = 0) -> dict:
    key = jax.random.key(seed)
    ks = jax.random.split(key, 27)

    def nrm(i, shape, scale):
        return jax.random.normal(ks[i], shape, jnp.float32) * scale

    d = D_MODEL
    dt = jnp.exp(jax.random.uniform(ks[14], (DEPTH, N_DIR, GDN_HEADS), jnp.float32,
                                    minval=math.log(1e-3), maxval=math.log(1e-1)))
    return {
        "x": nrm(0, (BATCH, SEQ, d), 1.0),
        "c": nrm(1, (BATCH, d), 1.0),
        "ctx": nrm(2, (BATCH, CTX_LEN, d), 1.0),
        "c_ctx": nrm(3, (d,), 1.0),
        "w_ada": nrm(4, (DEPTH, d, ADA_MULT * d), 0.5 * d ** -0.5),
        "b_ada": nrm(5, (DEPTH, ADA_MULT * d), 0.01),
        "w_in": nrm(6, (DEPTH, d, IN_COLS), d ** -0.5),
        "q_norm": 1.0 + nrm(7, (DEPTH, MLA_Q_RANK), 0.02),
        "kv_norm": 1.0 + nrm(8, (DEPTH, MLA_KV_RANK), 0.02),
        "w_uq": nrm(9, (DEPTH, MLA_Q_RANK, MLA_HEADS * MLA_QK), MLA_Q_RANK ** -0.5),
        "w_ukv": nrm(10, (DEPTH, MLA_KV_RANK, MLA_HEADS * (MLA_NOPE + MLA_V)), MLA_KV_RANK ** -0.5),
        "sc_conv": nrm(11, (DEPTH, SC_CONV_W, SC_WIDTH), SC_CONV_W ** -0.5),
        "gdn_conv": nrm(12, (DEPTH, GDN_CONV_W, 2 * GDN_QK + GDN_VW), GDN_CONV_W ** -0.5),
        "gdn_a_log": jnp.log(jax.random.uniform(ks[13], (DEPTH, N_DIR, GDN_HEADS), jnp.float32, minval=1.0, maxval=16.0)),
        "gdn_dt_bias": dt + jnp.log(-jnp.expm1(-dt)),
        "gdn_norm": 1.0 + nrm(15, (DEPTH, GDN_DV), 0.02),
        "w_branch": nrm(16, (DEPTH, N_BRANCH, BRANCH_W, d), BRANCH_W ** -0.5),
        "w_o": nrm(17, (DEPTH, d, d), DEEPNORM_BETA * d ** -0.5),
        "ln1_g": 1.0 + nrm(18, (DEPTH, d), 0.02),
        "ln1_b": nrm(19, (DEPTH, d), 0.01),
        "ln2_g": 1.0 + nrm(20, (DEPTH, d), 0.02),
        "ln2_b": nrm(21, (DEPTH, d), 0.01),
        "w_router": nrm(22, (d, N_EXPERTS), d ** -0.5),
        "router_bias": nrm(23, (N_EXPERTS,), 0.01),
        "w_e_gate": nrm(24, (DEPTH, N_EXPERTS, d, D_EXPERT), d ** -0.5),
        "w_e_up": nrm(25, (DEPTH, N_EXPERTS, d, D_EXPERT), d ** -0.5),
        "w_e_down": nrm(26, (DEPTH, N_EXPERTS, D_EXPERT, d), DEEPNORM_BETA * D_EXPERT ** -0.5),
    }


def reference(x, c, ctx, c_ctx, w_ada, b_ada, w_in, q_norm, kv_norm, w_uq, w_ukv, sc_conv, gdn_conv,
              gdn_a_log, gdn_dt_bias, gdn_norm, w_branch, w_o, ln1_g, ln1_b, ln2_g, ln2_b,
              w_router, router_bias, w_e_gate, w_e_up, w_e_down):
    bsz, n_lat, d = x.shape
    rope = axial_rope_tables(n_lat)
    x_lat, x_ctx = x, ctx
    for l in range(DEPTH):
        with_ctx = l < DEPTH - 1
        mod_lat = jnp.split((jax.nn.silu(c) @ w_ada[l] + b_ada[l])[:, None, :], ADA_MULT, axis=-1)
        mod_ctx = jnp.split(jax.nn.silu(c_ctx) @ w_ada[l] + b_ada[l], ADA_MULT, axis=-1)
        shift1, scale1, gate1, shift2, scale2, gate2 = mod_lat
        cshift1, cscale1, cgate1, cshift2, cscale2, cgate2 = mod_ctx
        mix_lat, mix_ctx = token_mixer(
            modulate(x_lat, shift1, scale1), modulate(x_ctx, cshift1, cscale1), rope,
            w_in[l], q_norm[l], kv_norm[l], w_uq[l], w_ukv[l], sc_conv[l], gdn_conv[l],
            gdn_a_log[l], gdn_dt_bias[l], gdn_norm[l], w_branch[l], w_o[l], with_ctx)
        x_lat = layer_norm(DEEPNORM_ALPHA * x_lat + gate1 * mix_lat, ln1_g[l], ln1_b[l])
        h_lat = modulate(x_lat, shift2, scale2).reshape(-1, d)
        if with_ctx:
            x_ctx = layer_norm(DEEPNORM_ALPHA * x_ctx + cgate1 * mix_ctx, ln1_g[l], ln1_b[l])
            h_ctx = modulate(x_ctx, cshift2, cscale2).reshape(-1, d)
            ffn = routed_moe(jnp.concatenate([h_lat, h_ctx], axis=0), w_router, router_bias,
                             w_e_gate[l], w_e_up[l], w_e_down[l])
            ffn_lat = ffn[:bsz * n_lat].reshape(x_lat.shape)
            ffn_ctx = ffn[bsz * n_lat:].reshape(x_ctx.shape)
            x_ctx = layer_norm(DEEPNORM_ALPHA * x_ctx + cgate2 * ffn_ctx, ln2_g[l], ln2_b[l])
        else:
            ffn_lat = routed_moe(h_lat, w_router, router_bias, w_e_gate[l], w_e_up[l], w_e_down[l]).reshape(x_lat.shape)
        x_lat = layer_norm(DEEPNORM_ALPHA * x_lat + gate2 * ffn_lat, ln2_g[l], ln2_b[l])
    return x_lat
```

```python
import functools
import math

import jax
import jax.numpy as jnp
from jax import lax
from jax.experimental import pallas as pl
from jax.experimental.pallas import tpu as pltpu

F32 = jnp.float32
BF16 = jnp.bfloat16

GRID_W = 64
MLA_HEADS = 16
MLA_Q_RANK = 1024
MLA_KV_RANK = 512
MLA_NOPE = 128
MLA_ROPE = 64
MLA_V = 128
MLA_QK = MLA_NOPE + MLA_ROPE
MLA_PAD = 256
ROPE_THETA = 10000.0
SC_WIDTH = 2048
GDN_HEADS = 16
GDN_DK = 128
GDN_DV = 128
GDN_QK = GDN_HEADS * GDN_DK
GDN_VW = GDN_HEADS * GDN_DV
GDN_CHUNK = 64
N_DIR = 2
BRANCH_W = 2048
N_BRANCH = 3
ADA_MULT = 6
N_EXPERTS = 16
N_GROUPS = 4
EXPERTS_PER_GROUP = N_EXPERTS // N_GROUPS
D_EXPERT = 1024
EPS = 1e-6
SMALL_W = MLA_Q_RANK + MLA_KV_RANK + 128
KV_LHS_W = MLA_KV_RANK + 128

VMEM_LIMIT = 56 * 1024 * 1024
LANES = 128


def _cparams(sem):
    return pltpu.CompilerParams(dimension_semantics=sem, vmem_limit_bytes=VMEM_LIMIT)


def _tile(dim, pref, align):
    if dim <= pref:
        return dim
    t = (pref // align) * align
    while t >= align:
        if dim % t == 0:
            return t
        t -= align
    return dim


def _silu(v):
    return v * jax.nn.sigmoid(v)


def _mm_kernel(*refs, nk, n_extra, epilogue, lhs_silu):
    a_ref, b_ref = refs[0], refs[1]
    extra = refs[2:2 + n_extra]
    o_ref = refs[2 + n_extra]
    a = a_ref[...]
    if lhs_silu:
        a = _silu(a.astype(F32))
    part = jnp.dot(a.astype(BF16), b_ref[...].astype(BF16), preferred_element_type=F32)
    if nk == 1:
        o_ref[...] = epilogue(part, *extra).astype(o_ref.dtype)
        return
    acc_ref = refs[3 + n_extra]
    k = pl.program_id(2)

    @pl.when(k == 0)
    def _():
        acc_ref[...] = part

    @pl.when(k > 0)
    def _():
        acc_ref[...] += part

    @pl.when(k == nk - 1)
    def _():
        o_ref[...] = epilogue(acc_ref[...], *extra).astype(o_ref.dtype)


def _matmul(a, b, *, out_dtype, m=None, tm=512, tn=512, tk=None, b_layer=None, extra=(),
            epilogue=None, lhs_silu=False, n_outer=False, name="matmul"):
    m = a.shape[0] if m is None else m
    kdim = a.shape[1]
    n = b.shape[-1]
    tm = _tile(m, tm, 8)
    tn = _tile(n, tn, LANES)
    tk = kdim if tk is None else _tile(kdim, tk, LANES)
    nk = kdim // tk
    if epilogue is None:
        epilogue = lambda acc: acc
    if n_outer:
        grid = (n // tn, m // tm, nk)
        ij = lambda g0, g1: (g1, g0)
    else:
        grid = (m // tm, n // tn, nk)
        ij = lambda g0, g1: (g0, g1)
    a_spec = pl.BlockSpec((tm, tk), lambda g0, g1, k: (ij(g0, g1)[0], k))
    if b.ndim == 3:
        b_spec = pl.BlockSpec((None, tk, tn), lambda g0, g1, k: (b_layer, k, ij(g0, g1)[1]))
    else:
        b_spec = pl.BlockSpec((tk, tn), lambda g0, g1, k: (k, ij(g0, g1)[1]))
    extra_specs = [pl.BlockSpec(bs, (lambda g0, g1, k, im=im: im(*ij(g0, g1)))) for _, bs, im in extra]
    scratch = [pltpu.VMEM((tm, tn), F32)] if nk > 1 else []
    return pl.pallas_call(
        functools.partial(_mm_kernel, nk=nk, n_extra=len(extra), epilogue=epilogue, lhs_silu=lhs_silu),
        grid=grid,
        in_specs=[a_spec, b_spec] + extra_specs,
        out_specs=pl.BlockSpec((tm, tn), lambda g0, g1, k: ij(g0, g1)),
        out_shape=jax.ShapeDtypeStruct((m, n), out_dtype),
        scratch_shapes=scratch,
        compiler_params=_cparams(("parallel", "parallel", "arbitrary")),
        name=name,
    )(a, b, *[e[0] for e in extra])


def _ep_bias(acc, bias_ref):
    return acc + bias_ref[...]


def _ep_sigmoid(acc):
    return jax.nn.sigmoid(acc)


def _ep_rope(acc, c_ref, s1_ref, s2_ref):
    n = acc.shape[1]
    reps = n // c_ref.shape[1]
    tile = lambda r: jnp.concatenate([r[...]] * reps, axis=1) if reps > 1 else r[...]
    return (acc * tile(c_ref) + pltpu.roll(acc, 16, 1) * tile(s1_ref)
            + pltpu.roll(acc, n - 16, 1) * tile(s2_ref))


def _mod_row_map(tm, n_lat, t_lat, n_batch):
    return lambda i: (jnp.minimum((i * tm) // t_lat, n_batch), 0, 0)


def _modulate_kernel(x_ref, mod_ref, h_ref, *, shift, scale):
    x = x_ref[...]
    h_ref[...] = (x * (1.0 + mod_ref[scale:scale + 1, :]) + mod_ref[shift:shift + 1, :]).astype(h_ref.dtype)


def _modulate(xa, mod, *, shift, scale, t_lat, n_batch, tm=256):
    m, d = xa.shape
    tm = _tile(math.gcd(m, t_lat), tm, 8)
    return pl.pallas_call(
        functools.partial(_modulate_kernel, shift=shift, scale=scale),
        grid=(m // tm,),
        in_specs=[pl.BlockSpec((tm, d), lambda i: (i, 0)),
                  pl.BlockSpec((None, ADA_MULT, d), _mod_row_map(tm, None, t_lat, n_batch))],
        out_specs=pl.BlockSpec((tm, d), lambda i: (i, 0)),
        out_shape=jax.ShapeDtypeStruct((m, d), BF16),
        compiler_params=_cparams(("parallel",)),
        name="modulate",
    )(xa, mod)


def _ln_kernel(*refs, alpha, gate, shift, scale, with_h, with_router):
    x_ref, y_ref, modg_ref, lnw_ref, lnb_ref = refs[:5]
    pos = 5
    modn_ref = wr_ref = None
    if with_h:
        modn_ref = refs[pos]
        pos += 1
    if with_router:
        wr_ref = refs[pos]
        pos += 1
    xo_ref = refs[pos]
    pos += 1
    v = alpha * x_ref[...] + modg_ref[gate:gate + 1, :] * y_ref[...]
    mean = jnp.mean(v, axis=-1, keepdims=True)
    vc = v - mean
    var = jnp.mean(vc * vc, axis=-1, keepdims=True)
    xn = vc * lax.rsqrt(var + EPS) * lnw_ref[...] + lnb_ref[...]
    xo_ref[...] = xn
    if with_h:
        h_ref = refs[pos]
        pos += 1
        h = xn * (1.0 + modn_ref[scale:scale + 1, :]) + modn_ref[shift:shift + 1, :]
        h_ref[...] = h.astype(h_ref.dtype)
        if with_router:
            lg_ref = refs[pos]
            lg_ref[...] = lax.dot_general(wr_ref[...], h, (((1,), (1,)), ((), ())),
                                          precision=lax.Precision.HIGHEST, preferred_element_type=F32)


def _layer_norm(xa, y, mod_g, lnw, lnb, *, m, alpha, gate, t_lat, n_batch, mod_n=None, shift=0, scale=0,
                w_router_t=None, tm=256):
    d = xa.shape[1]
    tm = _tile(math.gcd(m, t_lat), tm, LANES)
    with_h = mod_n is not None
    with_router = w_router_t is not None
    row = lambda i: (i, 0)
    modmap = _mod_row_map(tm, None, t_lat, n_batch)
    in_specs = [pl.BlockSpec((tm, d), row), pl.BlockSpec((tm, d), row),
                pl.BlockSpec((None, ADA_MULT, d), modmap),
                pl.BlockSpec((1, d), lambda i: (0, 0)), pl.BlockSpec((1, d), lambda i: (0, 0))]
    args = [xa, y, mod_g, lnw, lnb]
    out_specs = [pl.BlockSpec((tm, d), row)]
    out_shape = [jax.ShapeDtypeStruct((m, d), F32)]
    if with_h:
        in_specs.append(pl.BlockSpec((None, ADA_MULT, d), modmap))
        args.append(mod_n)
        out_specs.append(pl.BlockSpec((tm, d), row))
        out_shape.append(jax.ShapeDtypeStruct((m, d), BF16))
    if with_router:
        e = w_router_t.shape[0]
        in_specs.append(pl.BlockSpec((e, d), lambda i: (0, 0)))
        args.append(w_router_t)
        out_specs.append(pl.BlockSpec((e, tm), lambda i: (0, i)))
        out_shape.append(jax.ShapeDtypeStruct((e, m), F32))
    return pl.pallas_call(
        functools.partial(_ln_kernel, alpha=alpha, gate=gate, shift=shift, scale=scale, with_h=with_h,
                          with_router=with_router),
        grid=(m // tm,),
        in_specs=in_specs, out_specs=out_specs, out_shape=out_shape,
        compiler_params=_cparams(("parallel",)),
        name="layer_norm",
    )(*args)


def _seq_edges(i, tm, n_lat, t_lat, t_ctx):
    row0 = i * tm
    in_lat = row0 < n_lat
    start = jnp.where(in_lat, row0 % t_lat == 0, (row0 - n_lat) % t_ctx == 0)
    end = jnp.where(in_lat, (row0 + tm) % t_lat == 0, (row0 + tm - n_lat) % t_ctx == 0)
    return start, end


def _conv3(u, prev_row, next_row, w_ref):
    tm = u.shape[0]
    rows = lax.broadcasted_iota(jnp.int32, u.shape, 0)
    um1 = jnp.where(rows == 0, prev_row, pltpu.roll(u, 1, 0))
    up1 = jnp.where(rows == tm - 1, next_row, pltpu.roll(u, tm - 1, 0))
    return w_ref[0:1, :] * um1 + w_ref[1:2, :] * u + w_ref[2:3, :] * up1


HALO = 16


def _halo_specs(tm, tc, col_of_j, n_rows):
    last = n_rows // HALO - 1
    prev = pl.BlockSpec((HALO, tc), lambda i, j: (jnp.maximum(i * (tm // HALO) - 1, 0), col_of_j(j)))
    nxt = pl.BlockSpec((HALO, tc), lambda i, j: (jnp.minimum((i + 1) * (tm // HALO), last), col_of_j(j)))
    return prev, nxt


def _sconv_kernel(b_ref, c_ref, x_ref, cp_ref, xp_ref, cn_ref, xn_ref, w_ref, o_ref, *, tm, n_lat, t_lat,
                  t_ctx):
    start, end = _seq_edges(pl.program_id(0), tm, n_lat, t_lat, t_ctx)
    u = c_ref[...].astype(F32) * x_ref[...].astype(F32)
    prev = cp_ref[HALO - 1:HALO, :].astype(F32) * xp_ref[HALO - 1:HALO, :].astype(F32)
    nxt = cn_ref[0:1, :].astype(F32) * xn_ref[0:1, :].astype(F32)
    prev = jnp.where(start, 0.0, prev)
    nxt = jnp.where(end, 0.0, nxt)
    o_ref[...] = (b_ref[...].astype(F32) * _conv3(u, prev, nxt, w_ref)).astype(o_ref.dtype)


def _short_conv(p_sc, w, *, m, n_lat, t_lat, t_ctx, tm=256, tc=512):
    n_rows = p_sc.shape[0]
    width = SC_WIDTH
    tc = _tile(width, tc, LANES)
    tm = _tile(math.gcd(t_lat, t_ctx), tm, HALO)
    nb = width // tc
    blk = lambda g: pl.BlockSpec((tm, tc), lambda i, j: (i, g * nb + j))
    cp, cn = _halo_specs(tm, tc, lambda j: nb + j, n_rows)
    xp, xn = _halo_specs(tm, tc, lambda j: 2 * nb + j, n_rows)
    return pl.pallas_call(
        functools.partial(_sconv_kernel, tm=tm, n_lat=n_lat, t_lat=t_lat, t_ctx=t_ctx),
        grid=(m // tm, nb),
        in_specs=[blk(0), blk(1), blk(2), cp, xp, cn, xn, pl.BlockSpec((3, tc), lambda i, j: (0, j))],
        out_specs=pl.BlockSpec((tm, tc), lambda i, j: (i, j)),
        out_shape=jax.ShapeDtypeStruct((m, width), BF16),
        compiler_params=_cparams(("parallel", "parallel")),
        name="short_conv",
    )(p_sc, p_sc, p_sc, p_sc, p_sc, p_sc, p_sc, w)


def _small_kernel(p_ref, qn_ref, kvn_ref, ck_ref, s1_ref, s2_ref, nexp_ref, dtb_ref, cq_ref, kv_ref, gb_ref):
    cq = p_ref[:, :MLA_Q_RANK]
    cq_ref[...] = (cq * lax.rsqrt(jnp.mean(cq * cq, axis=-1, keepdims=True) + EPS) * qn_ref[...]).astype(
        cq_ref.dtype)
    ckv = p_ref[:, MLA_Q_RANK:MLA_Q_RANK + MLA_KV_RANK]
    kv_ref[:, :MLA_KV_RANK] = (ckv * lax.rsqrt(jnp.mean(ckv * ckv, axis=-1, keepdims=True) + EPS)
                               * kvn_ref[...]).astype(kv_ref.dtype)
    x = p_ref[:, MLA_Q_RANK + MLA_KV_RANK:]
    rot = x * ck_ref[...] + pltpu.roll(x, 16, 1) * s1_ref[...] + pltpu.roll(x, LANES - 16, 1) * s2_ref[...]
    kv_ref[:, MLA_KV_RANK:] = rot.astype(kv_ref.dtype)
    z = x + dtb_ref[...]
    g = nexp_ref[...] * (jnp.maximum(z, 0.0) + jnp.log1p(jnp.exp(-jnp.abs(z))))
    tm = x.shape[0]
    rows = lax.broadcasted_iota(jnp.int32, x.shape, 0) % GDN_CHUNK
    lanes = lax.broadcasted_iota(jnp.int32, x.shape, 1)
    fwd = g
    rev = g
    s = 1
    while s < GDN_CHUNK:
        fwd = fwd + jnp.where(rows >= s, pltpu.roll(fwd, s, 0), 0.0)
        rev = rev + jnp.where(rows < GDN_CHUNK - s, pltpu.roll(rev, tm - s, 0), 0.0)
        s *= 2
    beta = jax.nn.sigmoid(x)
    gb_ref[...] = jnp.where(lanes < 64, 0.0, jnp.where(lanes < 80, fwd, jnp.where(lanes < 96, rev, beta)))


def _small_prologue(p_small, q_norm, kv_norm, ck, s1, s2, nexp, dtb, *, tm=256):
    m = p_small.shape[0]
    tm = _tile(m, tm, GDN_CHUNK)
    row = lambda i: (i, 0)
    vec = lambda w: pl.BlockSpec((1, w), lambda i: (0, 0))
    tab = pl.BlockSpec((tm, LANES), row)
    return pl.pallas_call(
        _small_kernel,
        grid=(m // tm,),
        in_specs=[pl.BlockSpec((tm, SMALL_W), row), vec(MLA_Q_RANK), vec(MLA_KV_RANK), tab, tab, tab,
                  vec(LANES), vec(LANES)],
        out_specs=[pl.BlockSpec((tm, MLA_Q_RANK), row), pl.BlockSpec((tm, KV_LHS_W), row),
                   pl.BlockSpec((tm, LANES), row)],
        out_shape=[jax.ShapeDtypeStruct((m, MLA_Q_RANK), BF16), jax.ShapeDtypeStruct((m, KV_LHS_W), BF16),
                   jax.ShapeDtypeStruct((m, LANES), F32)],
        compiler_params=_cparams(("parallel",)),
        name="small_prologue",
    )(p_small, q_norm, kv_norm, ck, s1, s2, nexp, dtb)


def _attn_kernel(*refs, with_lat):
    if with_lat:
        q_ref, kc_ref, vc_ref, kl_ref, vl_ref, o_ref = refs
    else:
        q_ref, kc_ref, vc_ref, o_ref = refs
    nt = (((1,), (1,)), ((), ()))
    q = q_ref[...]
    sc = lax.dot_general(q, kc_ref[...], nt, preferred_element_type=F32)
    mx = jnp.max(sc, axis=-1, keepdims=True)
    if with_lat:
        sl = lax.dot_general(q, kl_ref[...], nt, preferred_element_type=F32)
        mx = jnp.maximum(mx, jnp.max(sl, axis=-1, keepdims=True))
    pc = jnp.exp(sc - mx)
    den = jnp.sum(pc, axis=-1, keepdims=True)
    acc = jnp.dot(pc.astype(BF16), vc_ref[...], preferred_element_type=F32)
    if with_lat:
        pl_ = jnp.exp(sl - mx)
        den = den + jnp.sum(pl_, axis=-1, keepdims=True)
        acc = acc + jnp.dot(pl_.astype(BF16), vl_ref[...], preferred_element_type=F32)
    o_ref[...] = (acc / den).astype(o_ref.dtype)


def _attention(q, kv, *, n_batch, t_lat, t_ctx, latent_queries, tq=256):
    n_lat = n_batch * t_lat
    h = MLA_HEADS
    vcol0 = h * MLA_PAD // MLA_V
    if latent_queries:
        tq = _tile(t_lat, tq, 8)
        nq = t_lat // tq
        m_out = n_lat
        qrow = lambda b, hh, qi: (b * nq + qi, hh)
        orow = qrow
    else:
        tq = t_ctx
        nq = 1
        m_out = n_batch * t_ctx
        qrow = lambda b, hh, qi: (n_lat // t_ctx + b, hh)
        orow = lambda b, hh, qi: (b, hh)
    ctx_blk = n_lat // t_ctx
    in_specs = [pl.BlockSpec((tq, MLA_PAD), qrow),
                pl.BlockSpec((t_ctx, MLA_PAD), lambda b, hh, qi: (ctx_blk + b, hh)),
                pl.BlockSpec((t_ctx, MLA_V), lambda b, hh, qi: (ctx_blk + b, vcol0 + hh))]
    args = [q, kv, kv]
    if latent_queries:
        in_specs += [pl.BlockSpec((t_lat, MLA_PAD), lambda b, hh, qi: (b, hh)),
                     pl.BlockSpec((t_lat, MLA_V), lambda b, hh, qi: (b, vcol0 + hh))]
        args += [kv, kv]
    return pl.pallas_call(
        functools.partial(_attn_kernel, with_lat=latent_queries),
        grid=(n_batch, h, nq),
        in_specs=in_specs,
        out_specs=pl.BlockSpec((tq, MLA_V), orow),
        out_shape=jax.ShapeDtypeStruct((m_out, h * MLA_V), BF16),
        compiler_params=_cparams(("parallel", "parallel", "arbitrary")),
        name="attention_lat" if latent_queries else "attention_ctx",
    )(*args)


def _gdn_feat_kernel(x_ref, xp_ref, xn_ref, w_ref, o_ref, *, tm, tc, n_lat, t_lat, t_ctx):
    start, end = _seq_edges(pl.program_id(0), tm, n_lat, t_lat, t_ctx)
    j = pl.program_id(1)
    u = x_ref[...].astype(F32)
    prev = jnp.where(start, 0.0, xp_ref[HALO - 1:HALO, :].astype(F32))
    nxt = jnp.where(end, 0.0, xn_ref[0:1, :].astype(F32))
    f = _silu(_conv3(u, prev, nxt, w_ref))
    is_q = j < GDN_QK // tc
    is_v = j >= 2 * GDN_QK // tc
    post = jnp.where(is_q, GDN_DK ** -0.5, 1.0)
    for hh in range(tc // GDN_DK):
        fh = f[:, hh * GDN_DK:(hh + 1) * GDN_DK]
        inv = lax.rsqrt(jnp.sum(fh * fh, axis=-1, keepdims=True) + EPS) * post
        o_ref[:, hh * GDN_DK:(hh + 1) * GDN_DK] = fh * jnp.where(is_v, 1.0, inv)


def _gdn_features(p_gdn, w, *, n_lat, t_lat, t_ctx, tm=256, tc=512):
    n_rows = p_gdn.shape[0]
    width = 2 * GDN_QK + GDN_VW
    tc = _tile(GDN_QK, tc, GDN_DK)
    tm = _tile(math.gcd(t_lat, t_ctx), tm, HALO)
    xp, xn = _halo_specs(tm, tc, lambda j: j, n_rows)
    return pl.pallas_call(
        functools.partial(_gdn_feat_kernel, tm=tm, tc=tc, n_lat=n_lat, t_lat=t_lat, t_ctx=t_ctx),
        grid=(n_rows // tm, width // tc),
        in_specs=[pl.BlockSpec((tm, tc), lambda i, j: (i, j)), xp, xn,
                  pl.BlockSpec((3, tc), lambda i, j: (0, j))],
        out_specs=pl.BlockSpec((tm, tc), lambda i, j: (i, j)),
        out_shape=jax.ShapeDtypeStruct((n_rows, width), F32),
        compiler_params=_cparams(("parallel", "parallel")),
        name="gdn_features",
    )(p_gdn, p_gdn, p_gdn, w)


def _bdot(a, b):
    return jnp.dot(a.astype(BF16), b.astype(BF16), preferred_element_type=F32)


def _bdot_nt(a, b):
    return lax.dot_general(a.astype(BF16), b.astype(BF16), (((1,), (1,)), ((), ())),
                           preferred_element_type=F32)


def _bdot_tn(a, b):
    return lax.dot_general(a.astype(BF16), b.astype(BF16), (((0,), (0,)), ((), ())),
                           preferred_element_type=F32)


def _gdn_chunk(q, k, v, gc_col, gc_row, beta_col, gc_end, state, reverse):
    c = GDN_CHUNK
    ri = lax.broadcasted_iota(jnp.int32, (c, c), 0)
    ci = lax.broadcasted_iota(jnp.int32, (c, c), 1)
    incl = (ri <= ci) if reverse else (ri >= ci)
    strict = (ri < ci) if reverse else (ri > ci)
    decay = jnp.exp(jnp.where(incl, gc_col - gc_row, -jnp.inf))
    kb = k * beta_col
    x = -jnp.where(strict, _bdot_nt(kb, k) * decay, 0.0)
    eye = jnp.where(ri == ci, 1.0, 0.0)
    tinv = eye + x
    step = 2
    while step < c:
        x = _bdot(x, x)
        tinv = tinv + _bdot(tinv, x)
        step *= 2
    egc = jnp.exp(gc_col)
    u = _bdot(tinv, v * beta_col)
    w = _bdot(tinv, kb * egc)
    v_new = u - _bdot(w, state)
    intra = _bdot_nt(q, k) * decay
    out = _bdot(q * egc, state) + _bdot(intra, v_new)
    k_tail = k * jnp.exp(gc_end - gc_col)
    new_state = state * jnp.exp(gc_end) + _bdot_tn(k_tail, v_new)
    return out, new_state


def _gdn_scan_kernel(f0_q, f0_k, f0_v, gb0_ref, gt0_ref, f1_q, f1_k, f1_v, gb1_ref, gt1_ref, o0_ref, o1_ref,
                     state_ref, *, hg):
    @pl.when(pl.program_id(2) == 0)
    def _():
        state_ref[...] = jnp.zeros_like(state_ref)

    head0 = pl.program_id(1) * hg
    for d, (fq, fk, fv, gb_ref, gt_ref, o_ref) in enumerate(
            ((f0_q, f0_k, f0_v, gb0_ref, gt0_ref, o0_ref), (f1_q, f1_k, f1_v, gb1_ref, gt1_ref, o1_ref))):
        for hh in range(hg):
            sl = slice(hh * GDN_DK, (hh + 1) * GDN_DK)
            gsel = gt_ref[pl.ds(64 + d * GDN_HEADS + head0 + hh, 1), :]
            lanes = lax.broadcasted_iota(jnp.int32, (GDN_CHUNK, LANES), 1)
            gbv = gb_ref[...]
            gc_col = jnp.sum(jnp.where(lanes == 64 + d * GDN_HEADS + head0 + hh, gbv, 0.0), axis=1,
                             keepdims=True)
            beta_col = jnp.sum(jnp.where(lanes == 96 + d * GDN_HEADS + head0 + hh, gbv, 0.0), axis=1,
                               keepdims=True)
            gc_end = gsel[:, 0:1] if d == 1 else gsel[:, GDN_CHUNK - 1:GDN_CHUNK]
            out, new_state = _gdn_chunk(fq[:, sl], fk[:, sl], fv[:, sl], gc_col, gsel, beta_col, gc_end,
                                        state_ref[d, hh], reverse=(d == 1))
            o_ref[:, sl] = out
            state_ref[d, hh] = new_state


def _gdn_scan(feat, gb, gbt, *, n_batch, t_lat, t_ctx, hg=4):
    n_rows = feat.shape[0]
    c = GDN_CHUNK
    n_lat = n_batch * t_lat
    nc_ctx, nc_lat = t_ctx // c, t_lat // c
    steps = nc_ctx + nc_lat
    tcw = hg * GDN_DK
    ng = GDN_HEADS // hg
    qb, kb_, vb = 0, GDN_QK // tcw, 2 * GDN_QK // tcw

    def chunk_row(d):
        def f(b, s):
            if d == 0:
                return jnp.where(s < nc_ctx, n_lat // c + b * nc_ctx + s, b * nc_lat + (s - nc_ctx))
            return jnp.where(s < nc_ctx, n_lat // c + b * nc_ctx + (nc_ctx - 1 - s),
                             b * nc_lat + (nc_lat - 1 - (s - nc_ctx)))
        return f

    in_specs, args = [], []
    for d in range(N_DIR):
        cr = chunk_row(d)
        for col0 in (qb, kb_, vb):
            in_specs.append(pl.BlockSpec((c, tcw), lambda b, g, s, cr=cr, col0=col0: (cr(b, s), col0 + g)))
            args.append(feat)
        in_specs.append(pl.BlockSpec((c, LANES), lambda b, g, s, cr=cr: (cr(b, s), 0)))
        args.append(gb)
        in_specs.append(pl.BlockSpec((None, LANES, c), lambda b, g, s, cr=cr: (cr(b, s), 0, 0)))
        args.append(gbt)
    out_specs = [pl.BlockSpec((c, tcw), lambda b, g, s, cr=chunk_row(d): (cr(b, s), g)) for d in range(N_DIR)]
    return pl.pallas_call(
        functools.partial(_gdn_scan_kernel, hg=hg),
        grid=(n_batch, ng, steps),
        in_specs=in_specs, out_specs=out_specs,
        out_shape=[jax.ShapeDtypeStruct((n_rows, GDN_VW), F32)] * N_DIR,
        scratch_shapes=[pltpu.VMEM((N_DIR, hg, GDN_DK, GDN_DV), F32)],
        compiler_params=_cparams(("parallel", "parallel", "arbitrary")),
        name="gdn_scan",
    )(*args)


def _gdn_out_kernel(o0_ref, o1_ref, z_ref, w_ref, y_ref):
    o = o0_ref[...] + o1_ref[...]
    z = z_ref[...].astype(F32)
    for hh in range(o.shape[1] // GDN_DV):
        sl = slice(hh * GDN_DV, (hh + 1) * GDN_DV)
        oh = o[:, sl]
        yh = oh * lax.rsqrt(jnp.mean(oh * oh, axis=-1, keepdims=True) + EPS) * w_ref[...]
        y_ref[:, sl] = (yh * _silu(z[:, sl])).astype(y_ref.dtype)


def _gdn_output(o0, o1, p_gdn, w, *, m, tm=256, tc=512):
    tc = _tile(GDN_VW, tc, GDN_DV)
    tm = _tile(m, tm, 16)
    z0 = (2 * GDN_QK + GDN_VW) // tc
    blk = pl.BlockSpec((tm, tc), lambda i, j: (i, j))
    return pl.pallas_call(
        _gdn_out_kernel,
        grid=(m // tm, GDN_VW // tc),
        in_specs=[blk, blk, pl.BlockSpec((tm, tc), lambda i, j: (i, z0 + j)),
                  pl.BlockSpec((1, GDN_DV), lambda i, j: (0, 0))],
        out_specs=blk,
        out_shape=jax.ShapeDtypeStruct((m, GDN_VW), BF16),
        compiler_params=_cparams(("parallel", "parallel")),
        name="gdn_output",
    )(o0, o1, p_gdn, w)


def _merge_kernel(a0, a1, a2, w_ref, g0, g1, g2, o_ref):
    acc = g0[...].astype(F32) * jnp.dot(a0[...], w_ref[0], preferred_element_type=F32)
    acc += g1[...].astype(F32) * jnp.dot(a1[...], w_ref[1], preferred_element_type=F32)
    acc += g2[...].astype(F32) * jnp.dot(a2[...], w_ref[2], preferred_element_type=F32)
    o_ref[...] = acc.astype(o_ref.dtype)


def _merge(attn, conv, gdn, w_branch, gates, *, m, tm=512, tn=512):
    d = w_branch.shape[-1]
    tm = _tile(m, tm, 16)
    tn = _tile(d, tn, LANES)
    nb = d // tn
    lhs = pl.BlockSpec((tm, BRANCH_W), lambda j, i: (i, 0))
    gate = lambda n: pl.BlockSpec((tm, tn), lambda j, i: (i, n * nb + j))
    return pl.pallas_call(
        _merge_kernel,
        grid=(nb, m // tm),
        in_specs=[lhs, lhs, lhs, pl.BlockSpec((N_BRANCH, BRANCH_W, tn), lambda j, i: (0, 0, j)),
                  gate(0), gate(1), gate(2)],
        out_specs=pl.BlockSpec((tm, tn), lambda j, i: (i, j)),
        out_shape=jax.ShapeDtypeStruct((m, d), BF16),
        compiler_params=_cparams(("parallel", "parallel")),
        name="merge",
    )(attn, conv, gdn, w_branch, gates, gates, gates)


def _route_kernel(lg_ref, bias_ref, comb_ref):
    aff = jax.nn.sigmoid(lg_ref[...])
    sel = aff + bias_ref[...]
    e, tn = aff.shape
    eidx = lax.broadcasted_iota(jnp.int32, (e, tn), 0)
    best = None
    for g in range(N_GROUPS):
        r = [sel[g * EXPERTS_PER_GROUP + t:g * EXPERTS_PER_GROUP + t + 1, :] for t in range(EXPERTS_PER_GROUP)]
        top2 = None
        for a in range(EXPERTS_PER_GROUP):
            for b in range(a + 1, EXPERTS_PER_GROUP):
                s = r[a] + r[b]
                top2 = s if top2 is None else jnp.maximum(top2, s)
        if best is None:
            best, gidx = top2, jnp.zeros_like(top2, dtype=jnp.int32)
        else:
            better = top2 > best
            best = jnp.where(better, top2, best)
            gidx = jnp.where(better, g, gidx)
    masked = jnp.where(eidx // EXPERTS_PER_GROUP == gidx, sel, -jnp.inf)
    m1 = jnp.max(masked, axis=0, keepdims=True)
    i1 = jnp.min(jnp.where(masked == m1, eidx, e), axis=0, keepdims=True)
    masked2 = jnp.where(eidx == i1, -jnp.inf, masked)
    m2 = jnp.max(masked2, axis=0, keepdims=True)
    i2 = jnp.min(jnp.where(masked2 == m2, eidx, e), axis=0, keepdims=True)
    hot = (eidx == i1) | (eidx == i2)
    w = jnp.where(hot, aff, 0.0)
    comb = w / jnp.sum(w, axis=0, keepdims=True)
    pad = jnp.zeros((LANES - e, tn), F32)
    comb_ref[...] = jnp.concatenate([comb, pad], axis=0).T


def _route(logits, bias_col, *, tn=512):
    e, m = logits.shape
    tn = _tile(m, tn, LANES)
    return pl.pallas_call(
        _route_kernel,
        grid=(m // tn,),
        in_specs=[pl.BlockSpec((e, tn), lambda i: (0, i)), pl.BlockSpec((e, 1), lambda i: (0, 0))],
        out_specs=pl.BlockSpec((tn, LANES), lambda i: (i, 0)),
        out_shape=jax.ShapeDtypeStruct((m, LANES), F32),
        compiler_params=_cparams(("parallel",)),
        name="route",
    )(logits, bias_col)


def _moe_up_kernel(h_ref, wg_ref, wu_ref, comb_ref, o_ref, *, nj):
    e = pl.program_id(0) // nj
    h = h_ref[...]
    g = jnp.dot(h, wg_ref[...], preferred_element_type=F32)
    u = jnp.dot(h, wu_ref[...], preferred_element_type=F32)
    cb = comb_ref[...]
    lanes = lax.broadcasted_iota(jnp.int32, cb.shape, 1)
    cw = jnp.sum(jnp.where(lanes == e, cb, 0.0), axis=1, keepdims=True)
    o_ref[...] = (_silu(g) * u * cw).astype(o_ref.dtype)


def _moe_up(h, wg, wu, comb, *, m, tm=512, tn=512):
    ne, d, f = wg.shape
    tm = _tile(m, tm, 16)
    tn = _tile(f, tn, LANES)
    nj = f // tn
    wspec = pl.BlockSpec((None, d, tn), lambda ej, i: (ej // nj, 0, ej % nj))
    return pl.pallas_call(
        functools.partial(_moe_up_kernel, nj=nj),
        grid=(ne * nj, m // tm),
        in_specs=[pl.BlockSpec((tm, d), lambda ej, i: (i, 0)), wspec, wspec,
                  pl.BlockSpec((tm, LANES), lambda ej, i: (i, 0))],
        out_specs=pl.BlockSpec((tm, tn), lambda ej, i: (i, ej)),
        out_shape=jax.ShapeDtypeStruct((m, ne * f), BF16),
        compiler_params=_cparams(("parallel", "parallel")),
        name="moe_up",
    )(h, wg, wu, comb)


def _rope_tables(n_batch, t_lat, t_ctx, width, lane0, scale):
    nf = MLA_ROPE // 4
    pos = jnp.arange(t_lat)
    inv = ROPE_THETA ** (-jnp.arange(nf, dtype=F32) / nf)
    ang_r = (pos // GRID_W).astype(F32)[:, None] * inv
    ang_c = (pos % GRID_W).astype(F32)[:, None] * inv
    z = jnp.zeros((t_lat, nf), F32)
    cos = jnp.concatenate([jnp.cos(ang_r)] * 2 + [jnp.cos(ang_c)] * 2, axis=1)
    s1 = jnp.concatenate([z, jnp.sin(ang_r), z, jnp.sin(ang_c)], axis=1)
    s2 = jnp.concatenate([-jnp.sin(ang_r), z, -jnp.sin(ang_c), z], axis=1)
    base = 1.0 if lane0 > 0 else 0.0

    def place(rot, ctx_rot, fill):
        def rows(r, n):
            left = jnp.full((n, lane0), fill, F32)
            right = jnp.zeros((n, width - lane0 - MLA_ROPE), F32)
            return jnp.concatenate([left, r, right], axis=1)
        lat = rows(rot, t_lat)
        ctx = rows(jnp.broadcast_to(ctx_rot, (t_ctx, MLA_ROPE)), t_ctx)
        return jnp.concatenate([lat] * n_batch + [ctx] * n_batch, axis=0) * scale

    return (place(cos, jnp.ones((MLA_ROPE,), F32), base), place(s1, jnp.zeros((MLA_ROPE,), F32), 0.0),
            place(s2, jnp.zeros((MLA_ROPE,), F32), 0.0))


def _prep_w_in(w):
    o = {}
    off = 0
    for name, width in (("cq", MLA_Q_RANK), ("ckv", MLA_KV_RANK), ("krope", MLA_ROPE), ("sc", 3 * SC_WIDTH),
                        ("gdn", 2 * GDN_QK + 2 * GDN_VW), ("ab", 2 * N_DIR * GDN_HEADS)):
        o[name] = (off, width)
        off += width
    sl = lambda n: w[:, o[n][0]:o[n][0] + o[n][1]]
    small = jnp.concatenate([sl("cq"), sl("ckv"), sl("krope"), sl("ab")], axis=1).astype(BF16)
    return small, sl("sc").astype(BF16), sl("gdn").astype(BF16), w[:, off:].astype(BF16)


def _prep_w_uq(w):
    r = w.shape[0]
    w3 = w.reshape(r, MLA_HEADS, MLA_QK)
    w3 = jnp.concatenate([w3, jnp.zeros((r, MLA_HEADS, MLA_PAD - MLA_QK), w.dtype)], axis=-1)
    return w3.reshape(r, MLA_HEADS * MLA_PAD).astype(BF16)


def _prep_w_ukv(w):
    r = w.shape[0]
    w3 = w.reshape(r, MLA_HEADS, MLA_NOPE + MLA_V)
    k_top = jnp.concatenate([w3[..., :MLA_NOPE], jnp.zeros((r, MLA_HEADS, MLA_PAD - MLA_NOPE), w.dtype)], -1)
    v_top = w3[..., MLA_NOPE:]
    eye = jnp.eye(MLA_ROPE, dtype=w.dtype)[:, None, :]
    k_rope = jnp.concatenate([jnp.zeros((MLA_ROPE, MLA_HEADS, MLA_NOPE), w.dtype),
                              jnp.broadcast_to(eye, (MLA_ROPE, MLA_HEADS, MLA_ROPE)),
                              jnp.zeros((MLA_ROPE, MLA_HEADS, MLA_PAD - MLA_QK), w.dtype)], -1)
    pad_rows = KV_LHS_W - r - MLA_ROPE
    keys = jnp.concatenate([k_top.reshape(r, -1), k_rope.reshape(MLA_ROPE, -1),
                            jnp.zeros((pad_rows, MLA_HEADS * MLA_PAD), w.dtype)], 0)
    vals = jnp.concatenate([v_top.reshape(r, -1), jnp.zeros((MLA_ROPE + pad_rows, MLA_HEADS * MLA_V), w.dtype)], 0)
    return jnp.concatenate([keys, vals], axis=1).astype(BF16)


def kernel(x, c, ctx, c_ctx, w_ada, b_ada, w_in, q_norm, kv_norm, w_uq, w_ukv, sc_conv, gdn_conv, gdn_a_log,
           gdn_dt_bias, gdn_norm, w_branch, w_o, ln1_g, ln1_b, ln2_g, ln2_b, w_router, router_bias, w_e_gate,
           w_e_up, w_e_down):
    n_batch, t_lat, d = x.shape
    t_ctx = ctx.shape[1]
    depth = w_ada.shape[0]
    n_lat = n_batch * t_lat
    n_all = n_lat + n_batch * t_ctx
    alpha = (2 * depth) ** 0.25
    geo = dict(t_lat=t_lat, n_batch=n_batch)

    xa = jnp.concatenate([x.reshape(n_lat, d), ctx.reshape(n_batch * t_ctx, d)], axis=0)
    cc = jnp.zeros((8, d), F32).at[:n_batch].set(c).at[n_batch].set(c_ctx)
    mods = [
        _matmul(cc, w_ada, b_layer=l, out_dtype=F32, tm=8, tn=512, lhs_silu=True,
                extra=((b_ada[l][None, :], (1, 512), lambda i, j: (0, j)),), epilogue=_ep_bias,
                name="adaln").reshape(8, ADA_MULT, d)
        for l in range(depth)
    ]
    q_tabs = _rope_tables(n_batch, t_lat, t_ctx, MLA_PAD, MLA_NOPE, MLA_QK ** -0.5)
    k_tabs = _rope_tables(n_batch, t_lat, t_ctx, LANES, 0, 1.0)
    w_router_t = w_router.T
    bias_col = router_bias.reshape(-1, 1)
    lane_vec = lambda v: jnp.zeros((1, LANES), F32).at[0, 64:96].set(v.reshape(-1))

    h1 = _modulate(xa, mods[0], shift=0, scale=1, **geo)
    for l in range(depth):
        with_ctx = l < depth - 1
        m = n_all if with_ctx else n_lat
        w_small, w_sc, w_gdn, w_gates = _prep_w_in(w_in[l])

        p_small = _matmul(h1, w_small, out_dtype=F32, name="in_small")
        p_sc = _matmul(h1, w_sc, out_dtype=BF16, m=m, name="in_sc")
        p_gdn = _matmul(h1, w_gdn, out_dtype=BF16, name="in_gdn")
        gates = _matmul(h1, w_gates, out_dtype=BF16, m=m, epilogue=_ep_sigmoid, name="in_gates")

        cqn, kvl, gb = _small_prologue(p_small, q_norm[l][None, :], kv_norm[l][None, :], *k_tabs,
                                       lane_vec(-jnp.exp(gdn_a_log[l])), lane_vec(gdn_dt_bias[l]))
        q = _matmul(cqn, _prep_w_uq(w_uq[l]), out_dtype=BF16, m=m, tn=MLA_PAD, epilogue=_ep_rope,
                    extra=tuple((t, (_tile(m, 512, 8), MLA_PAD), lambda i, j: (i, 0)) for t in q_tabs),
                    name="q_up")
        kv = _matmul(kvl, _prep_w_ukv(w_ukv[l]), out_dtype=BF16, name="kv_up")
        attn = _attention(q, kv, n_batch=n_batch, t_lat=t_lat, t_ctx=t_ctx, latent_queries=True)
        if with_ctx:
            attn_ctx = _attention(q, kv, n_batch=n_batch, t_lat=t_lat, t_ctx=t_ctx, latent_queries=False)
            attn = jnp.concatenate([attn, attn_ctx], axis=0)

        conv = _short_conv(p_sc, sc_conv[l], m=m, n_lat=n_lat, t_lat=t_lat, t_ctx=t_ctx)

        feat = _gdn_features(p_gdn, gdn_conv[l], n_lat=n_lat, t_lat=t_lat, t_ctx=t_ctx)
        gbt = jnp.swapaxes(gb.reshape(n_all // GDN_CHUNK, GDN_CHUNK, LANES), 1, 2)
        o0, o1 = _gdn_scan(feat, gb, gbt, n_batch=n_batch, t_lat=t_lat, t_ctx=t_ctx)
        gdn = _gdn_output(o0, o1, p_gdn, gdn_norm[l][None, :], m=m)

        merged = _merge(attn, conv, gdn, w_branch[l].astype(BF16), gates, m=m)
        mix = _matmul(merged, w_o[l].astype(BF16), out_dtype=F32, name="w_o")
        x1, h2, logits = _layer_norm(xa, mix, mods[l], ln1_g[l][None, :], ln1_b[l][None, :], m=m, alpha=alpha,
                                     gate=2, mod_n=mods[l], shift=3, scale=4, w_router_t=w_router_t, **geo)

        comb = _route(logits, bias_col)
        act = _moe_up(h2, w_e_gate[l].astype(BF16), w_e_up[l].astype(BF16), comb, m=m)
        ffn = _matmul(act, w_e_down[l].astype(BF16).reshape(-1, d), out_dtype=F32, tk=1024, name="moe_down")
        if with_ctx:
            xa, h1 = _layer_norm(x1, ffn, mods[l], ln2_g[l][None, :], ln2_b[l][None, :], m=m, alpha=alpha,
                                 gate=5, mod_n=mods[l + 1], shift=0, scale=1, **geo)
        else:
            (xa,) = _layer_norm(x1, ffn, mods[l], ln2_g[l][None, :], ln2_b[l][None, :], m=m, alpha=alpha,
                                gate=5, **geo)
    return xa[:n_lat].reshape(n_batch, t_lat, d)
```

```python
import functools
import math

import jax
import jax.numpy as jnp
from jax import lax
from jax.experimental import pallas as pl
from jax.experimental.pallas import tpu as pltpu

F32 = jnp.float32
BF16 = jnp.bfloat16

GRID_W = 64
MLA_HEADS = 16
MLA_Q_RANK = 1024
MLA_KV_RANK = 512
MLA_NOPE = 128
MLA_ROPE = 64
MLA_V = 128
MLA_QK = MLA_NOPE + MLA_ROPE
MLA_PAD = 256
ROPE_THETA = 10000.0
SC_WIDTH = 2048
GDN_HEADS = 16
GDN_DK = 128
GDN_DV = 128
GDN_QK = GDN_HEADS * GDN_DK
GDN_VW = GDN_HEADS * GDN_DV
GDN_CHUNK = 64
N_DIR = 2
QUAD = 4
BRANCH_W = 2048
N_BRANCH = 3
ADA_MULT = 6
N_EXPERTS = 16
N_GROUPS = 4
EXPERTS_PER_GROUP = N_EXPERTS // N_GROUPS
TOP_K = 2
EPS = 1e-6
SMALL_W = MLA_Q_RANK + MLA_KV_RANK + 128
KV_LHS_W = MLA_KV_RANK + 128

LANES = 128
SUBLANES = 8
VMEM_LIMIT = 56 * 1024 * 1024
MOE_TM = 256
DMA_WINDOW = 32
HALO = SUBLANES


def _cparams(sem):
    return pltpu.CompilerParams(dimension_semantics=sem, vmem_limit_bytes=VMEM_LIMIT)


def _tile(dim, pref, align):
    if dim <= pref:
        return dim
    t = (pref // align) * align
    while t >= align:
        if dim % t == 0:
            return t
        t -= align
    return dim


def _silu(v):
    return v * jax.nn.sigmoid(v)


def _mm_kernel(*refs, nk, n_extra, epilogue, lhs_silu):
    a_ref, b_ref = refs[0], refs[1]
    extra = refs[2:2 + n_extra]
    o_ref = refs[2 + n_extra]
    a = a_ref[...]
    if lhs_silu:
        a = _silu(a.astype(F32))
    part = jnp.dot(a.astype(BF16), b_ref[...].astype(BF16), preferred_element_type=F32)
    if nk == 1:
        o_ref[...] = epilogue(part, *extra).astype(o_ref.dtype)
        return
    acc_ref = refs[3 + n_extra]
    k = pl.program_id(2)

    @pl.when(k == 0)
    def _():
        acc_ref[...] = part

    @pl.when(k > 0)
    def _():
        acc_ref[...] += part

    @pl.when(k == nk - 1)
    def _():
        o_ref[...] = epilogue(acc_ref[...], *extra).astype(o_ref.dtype)


def _matmul(a, b, *, out_dtype, m=None, tm=512, tn=512, tk=None, b_layer=None, extra=(),
            epilogue=None, lhs_silu=False, cols_outer=False, name="matmul"):
    m = a.shape[0] if m is None else m
    kdim = a.shape[1]
    n = b.shape[-1]
    tm = _tile(m, tm, SUBLANES)
    tn = _tile(n, tn, LANES)
    tk = kdim if tk is None else _tile(kdim, tk, LANES)
    nk = kdim // tk
    if epilogue is None:
        epilogue = lambda acc: acc
    ij = (lambda g0, g1: (g1, g0)) if cols_outer else (lambda g0, g1: (g0, g1))
    grid = (n // tn, m // tm, nk) if cols_outer else (m // tm, n // tn, nk)
    a_spec = pl.BlockSpec((tm, tk), lambda g0, g1, k: (ij(g0, g1)[0], k))
    if b.ndim == 3:
        b_spec = pl.BlockSpec((None, tk, tn), lambda g0, g1, k: (b_layer, k, ij(g0, g1)[1]))
    else:
        b_spec = pl.BlockSpec((tk, tn), lambda g0, g1, k: (k, ij(g0, g1)[1]))
    extra_specs = [pl.BlockSpec(bs, (lambda g0, g1, k, im=im: im(*ij(g0, g1)))) for _, bs, im in extra]
    scratch = [pltpu.VMEM((tm, tn), F32)] if nk > 1 else []
    return pl.pallas_call(
        functools.partial(_mm_kernel, nk=nk, n_extra=len(extra), epilogue=epilogue, lhs_silu=lhs_silu),
        grid=grid,
        in_specs=[a_spec, b_spec] + extra_specs,
        out_specs=pl.BlockSpec((tm, tn), lambda g0, g1, k: ij(g0, g1)),
        out_shape=jax.ShapeDtypeStruct((m, n), out_dtype),
        scratch_shapes=scratch,
        compiler_params=_cparams(("parallel", "parallel", "arbitrary")),
        name=name,
    )(a, b, *[e[0] for e in extra])


def _ep_bias(acc, bias_ref):
    return acc + bias_ref[...]


def _ep_sigmoid(acc):
    return jax.nn.sigmoid(acc)


def _ep_rope(acc, c_ref, s1_ref, s2_ref):
    n = acc.shape[1]
    return (acc * c_ref[...] + pltpu.roll(acc, 16, 1) * s1_ref[...] + pltpu.roll(acc, n - 16, 1) * s2_ref[...])


def _mod_row_map(tm, t_lat, n_batch):
    return lambda i, *_: (jnp.minimum((i * tm) // t_lat, n_batch), 0, 0)


def _modulate_kernel(x_ref, mod_ref, h_ref, *, shift, scale):
    x = x_ref[...]
    h_ref[...] = (x * (1.0 + mod_ref[scale:scale + 1, :]) + mod_ref[shift:shift + 1, :]).astype(h_ref.dtype)


def _modulate(xa, mod, *, shift, scale, t_lat, n_batch, tm=256):
    m, d = xa.shape
    tm = _tile(math.gcd(m, t_lat), tm, SUBLANES)
    return pl.pallas_call(
        functools.partial(_modulate_kernel, shift=shift, scale=scale),
        grid=(m // tm,),
        in_specs=[pl.BlockSpec((tm, d), lambda i: (i, 0)),
                  pl.BlockSpec((None, ADA_MULT, d), _mod_row_map(tm, t_lat, n_batch))],
        out_specs=pl.BlockSpec((tm, d), lambda i: (i, 0)),
        out_shape=jax.ShapeDtypeStruct((m, d), BF16),
        compiler_params=_cparams(("parallel",)),
        name="modulate",
    )(xa, mod)


def _post_norm(x, y, gate_row, lnw_ref, lnb_ref, alpha):
    v = alpha * x + gate_row * y
    vc = v - jnp.mean(v, axis=-1, keepdims=True)
    var = jnp.mean(vc * vc, axis=-1, keepdims=True)
    return vc * lax.rsqrt(var + EPS) * lnw_ref[...] + lnb_ref[...]


def _ln_router_kernel(x_ref, y_ref, mod_ref, lnw_ref, lnb_ref, wr_ref, xo_ref, h_ref, lg_ref, *, alpha):
    xn = _post_norm(x_ref[...], y_ref[...], mod_ref[2:3, :], lnw_ref, lnb_ref, alpha)
    xo_ref[...] = xn
    h = xn * (1.0 + mod_ref[4:5, :]) + mod_ref[3:4, :]
    h_ref[...] = h
    lg_ref[...] = lax.dot_general(wr_ref[...], h.astype(BF16), (((1,), (1,)), ((), ())),
                                  preferred_element_type=F32)


def _ln_router(xa, y, mod, lnw, lnb, w_router_t, *, m, alpha, t_lat, n_batch, tm=256):
    d = xa.shape[1]
    e = w_router_t.shape[0]
    tm = _tile(math.gcd(m, t_lat), tm, LANES)
    row = lambda i: (i, 0)
    const = lambda i: (0, 0)
    return pl.pallas_call(
        functools.partial(_ln_router_kernel, alpha=alpha),
        grid=(m // tm,),
        in_specs=[pl.BlockSpec((tm, d), row), pl.BlockSpec((tm, d), row),
                  pl.BlockSpec((None, ADA_MULT, d), _mod_row_map(tm, t_lat, n_batch)),
                  pl.BlockSpec((1, d), const), pl.BlockSpec((1, d), const), pl.BlockSpec((e, d), const)],
        out_specs=[pl.BlockSpec((tm, d), row), pl.BlockSpec((tm, d), row), pl.BlockSpec((e, tm), lambda i: (0, i))],
        out_shape=[jax.ShapeDtypeStruct((m, d), F32), jax.ShapeDtypeStruct((m, d), F32),
                   jax.ShapeDtypeStruct((e, m), F32)],
        compiler_params=_cparams(("parallel",)),
        name="ln_router",
    )(xa, y, mod, lnw, lnb, w_router_t)


def _ln_combine_kernel(d0_ref, d1_ref, x_ref, y_ref, wt_ref, mod_ref, lnw_ref, lnb_ref, *rest, alpha, tm,
                       with_h):
    if with_h:
        modn_ref, xo_ref, h_ref, buf, sem = rest
    else:
        xo_ref, buf, sem = rest
    row0 = pl.program_id(0) * tm

    def row_copy(src_row, slot, r):
        return pltpu.make_async_copy(y_ref.at[pl.ds(src_row, 1)], buf.at[slot, pl.ds(r, 1)], sem)

    def issue(r, carry):
        row_copy(d0_ref[row0 + r], 0, r).start()
        row_copy(d1_ref[row0 + r], 1, r).start()
        return carry

    lax.fori_loop(0, tm, issue, 0)

    def wait(r, carry):
        row_copy(0, 0, 0).wait()
        row_copy(0, 1, 0).wait()
        return carry

    lax.fori_loop(0, tm, wait, 0)
    ffn = wt_ref[:, 0:1] * buf[0] + wt_ref[:, 1:2] * buf[1]
    xn = _post_norm(x_ref[...], ffn, mod_ref[5:6, :], lnw_ref, lnb_ref, alpha)
    xo_ref[...] = xn
    if with_h:
        h_ref[...] = (xn * (1.0 + modn_ref[1:2, :]) + modn_ref[0:1, :]).astype(h_ref.dtype)


def _ln_combine(x1, y_sorted, dest0, dest1, wt, mod, lnw, lnb, *, m, alpha, t_lat, n_batch, mod_next=None,
                tm=256):
    d = x1.shape[1]
    tm = _tile(math.gcd(m, t_lat), tm, SUBLANES)
    with_h = mod_next is not None
    row = lambda i, *_: (i, 0)
    const = lambda i, *_: (0, 0)
    modmap = _mod_row_map(tm, t_lat, n_batch)
    in_specs = [pl.BlockSpec((tm, d), row), pl.BlockSpec(memory_space=pl.ANY), pl.BlockSpec((tm, LANES), row),
                pl.BlockSpec((None, ADA_MULT, d), modmap), pl.BlockSpec((1, d), const), pl.BlockSpec((1, d), const)]
    args = [x1, y_sorted, wt, mod, lnw, lnb]
    out_specs = [pl.BlockSpec((tm, d), row)]
    out_shape = [jax.ShapeDtypeStruct((m, d), F32)]
    if with_h:
        in_specs.append(pl.BlockSpec((None, ADA_MULT, d), modmap))
        args.append(mod_next)
        out_specs.append(pl.BlockSpec((tm, d), row))
        out_shape.append(jax.ShapeDtypeStruct((m, d), BF16))
    return pl.pallas_call(
        functools.partial(_ln_combine_kernel, alpha=alpha, tm=tm, with_h=with_h),
        grid_spec=pltpu.PrefetchScalarGridSpec(
            num_scalar_prefetch=2, grid=(m // tm,), in_specs=in_specs, out_specs=out_specs,
            scratch_shapes=[pltpu.VMEM((TOP_K, tm, d), F32), pltpu.SemaphoreType.DMA(())]),
        out_shape=out_shape,
        compiler_params=_cparams(("arbitrary",)),
        name="ln_combine",
    )(dest0, dest1, *args)


def _seq_edges(i, tm, n_lat, t_lat, t_ctx):
    row0 = i * tm
    in_lat = row0 < n_lat
    start = jnp.where(in_lat, row0 % t_lat == 0, (row0 - n_lat) % t_ctx == 0)
    end = jnp.where(in_lat, (row0 + tm) % t_lat == 0, (row0 + tm - n_lat) % t_ctx == 0)
    return start, end


def _conv3(u, prev_row, next_row, w_ref):
    tm = u.shape[0]
    rows = lax.broadcasted_iota(jnp.int32, u.shape, 0)
    um1 = jnp.where(rows == 0, prev_row, pltpu.roll(u, 1, 0))
    up1 = jnp.where(rows == tm - 1, next_row, pltpu.roll(u, tm - 1, 0))
    return w_ref[0:1, :] * um1 + w_ref[1:2, :] * u + w_ref[2:3, :] * up1


def _halo_specs(tm, tc, col_of_j, n_rows):
    last = n_rows // HALO - 1
    prev = pl.BlockSpec((HALO, tc), lambda i, j: (jnp.maximum(i * (tm // HALO) - 1, 0), col_of_j(j)))
    nxt = pl.BlockSpec((HALO, tc), lambda i, j: (jnp.minimum((i + 1) * (tm // HALO), last), col_of_j(j)))
    return prev, nxt


def _sconv_kernel(b_ref, c_ref, x_ref, cp_ref, xp_ref, cn_ref, xn_ref, w_ref, o_ref, *, tm, n_lat, t_lat,
                  t_ctx):
    start, end = _seq_edges(pl.program_id(0), tm, n_lat, t_lat, t_ctx)
    u = c_ref[...] * x_ref[...]
    prev = jnp.where(start, 0.0, cp_ref[HALO - 1:HALO, :] * xp_ref[HALO - 1:HALO, :])
    nxt = jnp.where(end, 0.0, cn_ref[0:1, :] * xn_ref[0:1, :])
    o_ref[...] = (b_ref[...] * _conv3(u, prev, nxt, w_ref)).astype(o_ref.dtype)


def _short_conv(p_sc, w, *, m, n_lat, t_lat, t_ctx, tm=256, tc=512):
    n_rows = p_sc.shape[0]
    width = SC_WIDTH
    tc = _tile(width, tc, LANES)
    tm = _tile(math.gcd(t_lat, t_ctx), tm, HALO)
    nb = width // tc
    blk = lambda g: pl.BlockSpec((tm, tc), lambda i, j: (i, g * nb + j))
    cp, cn = _halo_specs(tm, tc, lambda j: nb + j, n_rows)
    xp, xn = _halo_specs(tm, tc, lambda j: 2 * nb + j, n_rows)
    return pl.pallas_call(
        functools.partial(_sconv_kernel, tm=tm, n_lat=n_lat, t_lat=t_lat, t_ctx=t_ctx),
        grid=(m // tm, nb),
        in_specs=[blk(0), blk(1), blk(2), cp, xp, cn, xn, pl.BlockSpec((3, tc), lambda i, j: (0, j))],
        out_specs=pl.BlockSpec((tm, tc), lambda i, j: (i, j)),
        out_shape=jax.ShapeDtypeStruct((m, width), BF16),
        compiler_params=_cparams(("parallel", "parallel")),
        name="short_conv",
    )(p_sc, p_sc, p_sc, p_sc, p_sc, p_sc, p_sc, w)


def _small_kernel(p_ref, qn_ref, kvn_ref, ck_ref, s1_ref, s2_ref, nexp_ref, dtb_ref, cq_ref, kv_ref, gb_ref):
    cq = p_ref[:, :MLA_Q_RANK]
    cq_ref[...] = (cq * lax.rsqrt(jnp.mean(cq * cq, axis=-1, keepdims=True) + EPS) * qn_ref[...]).astype(
        cq_ref.dtype)
    ckv = p_ref[:, MLA_Q_RANK:MLA_Q_RANK + MLA_KV_RANK]
    kv_ref[:, :MLA_KV_RANK] = (ckv * lax.rsqrt(jnp.mean(ckv * ckv, axis=-1, keepdims=True) + EPS)
                               * kvn_ref[...]).astype(kv_ref.dtype)
    x = p_ref[:, MLA_Q_RANK + MLA_KV_RANK:]
    rot = x * ck_ref[...] + pltpu.roll(x, 16, 1) * s1_ref[...] + pltpu.roll(x, LANES - 16, 1) * s2_ref[...]
    kv_ref[:, MLA_KV_RANK:] = rot.astype(kv_ref.dtype)
    z = x + dtb_ref[...]
    g = nexp_ref[...] * (jnp.maximum(z, 0.0) + jnp.log1p(jnp.exp(-jnp.abs(z))))
    tm = x.shape[0]
    rows = lax.broadcasted_iota(jnp.int32, x.shape, 0) % GDN_CHUNK
    lanes = lax.broadcasted_iota(jnp.int32, x.shape, 1)
    fwd = g
    rev = g
    s = 1
    while s < GDN_CHUNK:
        fwd = fwd + jnp.where(rows >= s, pltpu.roll(fwd, s, 0), 0.0)
        rev = rev + jnp.where(rows < GDN_CHUNK - s, pltpu.roll(rev, tm - s, 0), 0.0)
        s *= 2
    beta = jax.nn.sigmoid(x)
    gb_ref[...] = jnp.where(lanes < 64, 0.0, jnp.where(lanes < 80, fwd, jnp.where(lanes < 96, rev, beta)))


def _small_prologue(p_small, q_norm, kv_norm, ck, s1, s2, nexp, dtb, *, tm=256):
    m = p_small.shape[0]
    tm = _tile(m, tm, GDN_CHUNK)
    row = lambda i: (i, 0)
    vec = lambda w: pl.BlockSpec((1, w), lambda i: (0, 0))
    tab = pl.BlockSpec((tm, LANES), row)
    return pl.pallas_call(
        _small_kernel,
        grid=(m // tm,),
        in_specs=[pl.BlockSpec((tm, SMALL_W), row), vec(MLA_Q_RANK), vec(MLA_KV_RANK), tab, tab, tab,
                  vec(LANES), vec(LANES)],
        out_specs=[pl.BlockSpec((tm, MLA_Q_RANK), row), pl.BlockSpec((tm, KV_LHS_W), row),
                   pl.BlockSpec((tm, LANES), row)],
        out_shape=[jax.ShapeDtypeStruct((m, MLA_Q_RANK), BF16), jax.ShapeDtypeStruct((m, KV_LHS_W), BF16),
                   jax.ShapeDtypeStruct((m, LANES), F32)],
        compiler_params=_cparams(("parallel",)),
        name="small_prologue",
    )(p_small, q_norm, kv_norm, ck, s1, s2, nexp, dtb)


def _attn_kernel(*refs, with_lat):
    if with_lat:
        q_ref, kc_ref, vc_ref, kl_ref, vl_ref, o_ref = refs
    else:
        q_ref, kc_ref, vc_ref, o_ref = refs
    nt = (((1,), (1,)), ((), ()))
    q = q_ref[...]
    sc = lax.dot_general(q, kc_ref[...], nt, preferred_element_type=F32)
    mx = jnp.max(sc, axis=-1, keepdims=True)
    if with_lat:
        sl = lax.dot_general(q, kl_ref[...], nt, preferred_element_type=F32)
        mx = jnp.maximum(mx, jnp.max(sl, axis=-1, keepdims=True))
    pc = jnp.exp(sc - mx)
    den = jnp.sum(pc, axis=-1, keepdims=True)
    acc = jnp.dot(pc.astype(BF16), vc_ref[...], preferred_element_type=F32)
    if with_lat:
        pl_ = jnp.exp(sl - mx)
        den = den + jnp.sum(pl_, axis=-1, keepdims=True)
        acc = acc + jnp.dot(pl_.astype(BF16), vl_ref[...], preferred_element_type=F32)
    o_ref[...] = (acc / den).astype(o_ref.dtype)


def _attention(q, kv, *, n_batch, t_lat, t_ctx, latent_queries, tq=512):
    n_lat = n_batch * t_lat
    h = MLA_HEADS
    vcol0 = h * MLA_PAD // MLA_V
    if latent_queries:
        tq = _tile(t_lat, tq, SUBLANES)
        nq = t_lat // tq
        m_out = n_lat
        qrow = lambda b, hh, qi: (b * nq + qi, hh)
        orow = qrow
    else:
        tq = t_ctx
        nq = 1
        m_out = n_batch * t_ctx
        qrow = lambda b, hh, qi: (n_lat // t_ctx + b, hh)
        orow = lambda b, hh, qi: (b, hh)
    ctx_blk = n_lat // t_ctx
    in_specs = [pl.BlockSpec((tq, MLA_PAD), qrow),
                pl.BlockSpec((t_ctx, MLA_PAD), lambda b, hh, qi: (ctx_blk + b, hh)),
                pl.BlockSpec((t_ctx, MLA_V), lambda b, hh, qi: (ctx_blk + b, vcol0 + hh))]
    args = [q, kv, kv]
    if latent_queries:
        in_specs += [pl.BlockSpec((t_lat, MLA_PAD), lambda b, hh, qi: (b, hh)),
                     pl.BlockSpec((t_lat, MLA_V), lambda b, hh, qi: (b, vcol0 + hh))]
        args += [kv, kv]
    return pl.pallas_call(
        functools.partial(_attn_kernel, with_lat=latent_queries),
        grid=(n_batch, h, nq),
        in_specs=in_specs,
        out_specs=pl.BlockSpec((tq, MLA_V), orow),
        out_shape=jax.ShapeDtypeStruct((m_out, h * MLA_V), BF16),
        compiler_params=_cparams(("parallel", "parallel", "arbitrary")),
        name="attention_lat" if latent_queries else "attention_ctx",
    )(*args)


def _gdn_feat_kernel(x_ref, xp_ref, xn_ref, w_ref, o_ref, *, tm, tc, n_lat, t_lat, t_ctx):
    start, end = _seq_edges(pl.program_id(0), tm, n_lat, t_lat, t_ctx)
    j = pl.program_id(1)
    prev = jnp.where(start, 0.0, xp_ref[HALO - 1:HALO, :])
    nxt = jnp.where(end, 0.0, xn_ref[0:1, :])
    f = _silu(_conv3(x_ref[...], prev, nxt, w_ref))
    is_q = j < GDN_QK // tc
    is_v = j >= 2 * GDN_QK // tc
    post = jnp.where(is_q, GDN_DK ** -0.5, 1.0)
    for hh in range(tc // GDN_DK):
        fh = f[:, hh * GDN_DK:(hh + 1) * GDN_DK]
        inv = lax.rsqrt(jnp.sum(fh * fh, axis=-1, keepdims=True) + EPS) * post
        o_ref[:, hh * GDN_DK:(hh + 1) * GDN_DK] = fh * jnp.where(is_v, 1.0, inv)


def _gdn_features(p_gdn, w, *, n_lat, t_lat, t_ctx, tm=256, tc=512):
    n_rows = p_gdn.shape[0]
    width = 2 * GDN_QK + GDN_VW
    tc = _tile(GDN_QK, tc, GDN_DK)
    tm = _tile(math.gcd(t_lat, t_ctx), tm, HALO)
    xp, xn = _halo_specs(tm, tc, lambda j: j, n_rows)
    return pl.pallas_call(
        functools.partial(_gdn_feat_kernel, tm=tm, tc=tc, n_lat=n_lat, t_lat=t_lat, t_ctx=t_ctx),
        grid=(n_rows // tm, width // tc),
        in_specs=[pl.BlockSpec((tm, tc), lambda i, j: (i, j)), xp, xn,
                  pl.BlockSpec((3, tc), lambda i, j: (0, j))],
        out_specs=pl.BlockSpec((tm, tc), lambda i, j: (i, j)),
        out_shape=jax.ShapeDtypeStruct((n_rows, width), F32),
        compiler_params=_cparams(("parallel", "parallel")),
        name="gdn_features",
    )(p_gdn, p_gdn, p_gdn, w)


def _block_diag(xp, blk, n):
    return jnp.concatenate([jnp.where(blk == p, xp, jnp.zeros_like(xp)) for p in range(n)], axis=0)


def _gdn_scan_kernel(fq0, fk0, fv0, gb0, gr0, fq1, fk1, fv1, gb1, gr1, o0_ref, o1_ref, state_ref):
    c = GDN_CHUNK
    pw = QUAD * c
    nt = (((1,), (1,)), ((), ()))
    tn = (((0,), (0,)), ((), ()))

    @pl.when(pl.program_id(1) == 0)
    def _():
        state_ref[...] = jnp.zeros_like(state_ref)

    ri = lax.broadcasted_iota(jnp.int32, (c, pw), 0)
    ci = lax.broadcasted_iota(jnp.int32, (c, pw), 1)
    cj = ci % c
    blk = ci // c
    blk_k = lax.broadcasted_iota(jnp.int32, (c, QUAD * GDN_DK), 1) // GDN_DK
    eye_p = jnp.where(ri == cj, 1.0, 0.0)
    lanes = lax.broadcasted_iota(jnp.int32, (c, LANES), 1)
    dirs = ((fq0, fk0, fv0, gb0, gr0), (fq1, fk1, fv1, gb1, gr1))
    nq = GDN_HEADS // QUAD

    quads = []
    for d, (fq, fk, fv, gb_ref, gr_ref) in enumerate(dirs):
        incl = (ri <= cj) if d == 1 else (ri >= cj)
        strict = (ri < cj) if d == 1 else (ri > cj)
        gbv = gb_ref[...]
        for g in range(nq):
            cols = slice(g * QUAD * GDN_DK, (g + 1) * QUAD * GDN_DK)
            k4, q4, v4 = fk[:, cols], fq[:, cols], fv[:, cols]
            grow = gr_ref[d * nq + g:d * nq + g + 1, :]
            heads, kb_l, gcol_p = [], [], None
            for p in range(QUAD):
                h = g * QUAD + p
                hs = slice(p * GDN_DK, (p + 1) * GDN_DK)
                gcol = jnp.sum(jnp.where(lanes == 64 + d * GDN_HEADS + h, gbv, 0.0), axis=1, keepdims=True)
                bcol = jnp.sum(jnp.where(lanes == 96 + d * GDN_HEADS + h, gbv, 0.0), axis=1, keepdims=True)
                gend = grow[:, p * c:p * c + 1] if d == 1 else grow[:, (p + 1) * c - 1:(p + 1) * c]
                egc = jnp.exp(gcol)
                kp = k4[:, hs]
                kb = kp * bcol
                kb_l.append(kb)
                heads.append(dict(
                    h=h,
                    rhs=jnp.concatenate([v4[:, hs] * bcol, kb * egc], axis=1).astype(BF16),
                    qe=q4[:, hs] * egc,
                    kt=(kp * jnp.exp(gend - gcol)).astype(BF16),
                    cd=jnp.exp(gend),
                ))
                gcol_p = gcol if gcol_p is None else jnp.where(blk == p, gcol, gcol_p)
            decay = jnp.exp(jnp.where(incl, gcol_p - grow, -jnp.inf))
            kbd = _block_diag(k4.astype(BF16), blk_k, QUAD)
            lhs = jnp.concatenate([jnp.concatenate(kb_l, axis=1), q4], axis=0).astype(BF16)
            quads.append(dict(d=d, heads=heads, decay=decay, strict=strict, kbd=kbd, lhs=lhs))

    for qd in quads:
        sc = lax.dot_general(qd["lhs"], qd["kbd"], nt, preferred_element_type=F32)
        xp = -jnp.where(qd["strict"], sc[:c] * qd["decay"], 0.0)
        qd["intra"] = (sc[c:] * qd["decay"]).astype(BF16)
        qd["tinv"] = eye_p + xp
        qd["xb"] = xp.astype(BF16)
        qd["xbd"] = _block_diag(qd["xb"], blk, QUAD)

    step = 2
    while step < c:
        for qd in quads:
            qd["xb"] = jnp.dot(qd["xb"], qd["xbd"], preferred_element_type=F32).astype(BF16)
        for qd in quads:
            qd["xbd"] = _block_diag(qd["xb"], blk, QUAD)
        for qd in quads:
            qd["tinv"] = qd["tinv"] + jnp.dot(qd["tinv"].astype(BF16), qd["xbd"], preferred_element_type=F32)
        step *= 2

    for qd in quads:
        t16 = qd["tinv"].astype(BF16)
        for p, hd in enumerate(qd["heads"]):
            hd["uw"] = jnp.dot(t16[:, p * c:(p + 1) * c], hd["rhs"], preferred_element_type=F32)

    for qd in quads:
        for hd in qd["heads"]:
            hd["st"] = state_ref[qd["d"], hd["h"]]
            lhs = jnp.concatenate([hd["uw"][:, GDN_DV:], hd["qe"]], axis=0).astype(BF16)
            hd["ws"] = jnp.dot(lhs, hd["st"].astype(BF16), preferred_element_type=F32)
    for qd in quads:
        for hd in qd["heads"]:
            hd["vn"] = (hd["uw"][:, :GDN_DV] - hd["ws"][:c]).astype(BF16)
    outs = ([], [])
    for qd in quads:
        for p, hd in enumerate(qd["heads"]):
            outs[qd["d"]].append(hd["ws"][c:] + jnp.dot(qd["intra"][:, p * c:(p + 1) * c], hd["vn"],
                                                         preferred_element_type=F32))
            state_ref[qd["d"], hd["h"]] = hd["st"] * hd["cd"] + lax.dot_general(
                hd["kt"], hd["vn"], tn, preferred_element_type=F32)
    o0_ref[...] = jnp.concatenate(outs[0], axis=1)
    o1_ref[...] = jnp.concatenate(outs[1], axis=1)


def _gdn_scan(feat, gb, grows, *, n_batch, t_lat, t_ctx):
    n_rows = feat.shape[0]
    c = GDN_CHUNK
    n_lat = n_batch * t_lat
    nc_ctx, nc_lat = t_ctx // c, t_lat // c

    def chunk_row(d):
        def f(b, s):
            if d == 0:
                return jnp.where(s < nc_ctx, n_lat // c + b * nc_ctx + s, b * nc_lat + (s - nc_ctx))
            return jnp.where(s < nc_ctx, n_lat // c + b * nc_ctx + (nc_ctx - 1 - s),
                             b * nc_lat + (nc_lat - 1 - (s - nc_ctx)))
        return f

    in_specs, args = [], []
    for d in range(N_DIR):
        cr = chunk_row(d)
        for col0 in range(3):
            in_specs.append(pl.BlockSpec((c, GDN_QK), lambda b, s, cr=cr, col0=col0: (cr(b, s), col0)))
            args.append(feat)
        in_specs.append(pl.BlockSpec((c, LANES), lambda b, s, cr=cr: (cr(b, s), 0)))
        args.append(gb)
        in_specs.append(pl.BlockSpec((None,) + grows.shape[1:], lambda b, s, cr=cr: (cr(b, s), 0, 0)))
        args.append(grows)
    out_specs = [pl.BlockSpec((c, GDN_VW), lambda b, s, cr=chunk_row(d): (cr(b, s), 0)) for d in range(N_DIR)]
    return pl.pallas_call(
        _gdn_scan_kernel,
        grid=(n_batch, nc_ctx + nc_lat),
        in_specs=in_specs, out_specs=out_specs,
        out_shape=[jax.ShapeDtypeStruct((n_rows, GDN_VW), F32)] * N_DIR,
        scratch_shapes=[pltpu.VMEM((N_DIR, GDN_HEADS, GDN_DK, GDN_DV), F32)],
        compiler_params=_cparams(("parallel", "arbitrary")),
        name="gdn_scan",
    )(*args)


def _pack_decay_rows(gb):
    n_chunks = gb.shape[0] // GDN_CHUNK
    g = gb[:, 64:96].reshape(n_chunks, GDN_CHUNK, N_DIR, GDN_HEADS // QUAD, QUAD)
    return jnp.transpose(g, (0, 2, 3, 4, 1)).reshape(n_chunks, N_DIR * GDN_HEADS // QUAD, QUAD * GDN_CHUNK)


def _gdn_out_kernel(o0_ref, o1_ref, z_ref, w_ref, y_ref):
    o = o0_ref[...] + o1_ref[...]
    z = z_ref[...]
    for hh in range(o.shape[1] // GDN_DV):
        sl = slice(hh * GDN_DV, (hh + 1) * GDN_DV)
        oh = o[:, sl]
        yh = oh * lax.rsqrt(jnp.mean(oh * oh, axis=-1, keepdims=True) + EPS) * w_ref[...]
        y_ref[:, sl] = (yh * _silu(z[:, sl])).astype(y_ref.dtype)


def _gdn_output(o0, o1, p_gdn, w, *, m, tm=256, tc=512):
    tc = _tile(GDN_VW, tc, GDN_DV)
    tm = _tile(m, tm, 16)
    z0 = (2 * GDN_QK + GDN_VW) // tc
    blk = pl.BlockSpec((tm, tc), lambda i, j: (i, j))
    return pl.pallas_call(
        _gdn_out_kernel,
        grid=(m // tm, GDN_VW // tc),
        in_specs=[blk, blk, pl.BlockSpec((tm, tc), lambda i, j: (i, z0 + j)),
                  pl.BlockSpec((1, GDN_DV), lambda i, j: (0, 0))],
        out_specs=blk,
        out_shape=jax.ShapeDtypeStruct((m, GDN_VW), BF16),
        compiler_params=_cparams(("parallel", "parallel")),
        name="gdn_output",
    )(o0, o1, p_gdn, w)


def _merge_kernel(a0, a1, a2, w_ref, g0, g1, g2, o_ref):
    acc = g0[...].astype(F32) * jnp.dot(a0[...], w_ref[0], preferred_element_type=F32)
    acc += g1[...].astype(F32) * jnp.dot(a1[...], w_ref[1], preferred_element_type=F32)
    acc += g2[...].astype(F32) * jnp.dot(a2[...], w_ref[2], preferred_element_type=F32)
    o_ref[...] = acc.astype(o_ref.dtype)


def _merge(attn, conv, gdn, w_branch, gates, *, m, tm=512, tn=512):
    d = w_branch.shape[-1]
    tm = _tile(m, tm, 16)
    tn = _tile(d, tn, LANES)
    nb = d // tn
    lhs = pl.BlockSpec((tm, BRANCH_W), lambda j, i: (i, 0))
    gate = lambda n: pl.BlockSpec((tm, tn), lambda j, i: (i, n * nb + j))
    return pl.pallas_call(
        _merge_kernel,
        grid=(nb, m // tm),
        in_specs=[lhs, lhs, lhs, pl.BlockSpec((N_BRANCH, BRANCH_W, tn), lambda j, i: (0, 0, j)),
                  gate(0), gate(1), gate(2)],
        out_specs=pl.BlockSpec((tm, tn), lambda j, i: (i, j)),
        out_shape=jax.ShapeDtypeStruct((m, d), BF16),
        compiler_params=_cparams(("parallel", "parallel")),
        name="merge",
    )(attn, conv, gdn, w_branch, gates, gates, gates)


def _route_kernel(lg_ref, bias_ref, idx_ref, rank_ref, wt_ref, cnt_ref, run_ref):
    @pl.when(pl.program_id(0) == 0)
    def _():
        run_ref[...] = jnp.zeros_like(run_ref)

    aff = jax.nn.sigmoid(lg_ref[...])
    sel = aff + bias_ref[...]
    e, tn = aff.shape
    eidx = lax.broadcasted_iota(jnp.int32, (e, tn), 0)
    best = None
    for g in range(N_GROUPS):
        r = [sel[g * EXPERTS_PER_GROUP + t:g * EXPERTS_PER_GROUP + t + 1, :] for t in range(EXPERTS_PER_GROUP)]
        top2 = None
        for a in range(EXPERTS_PER_GROUP):
            for b in range(a + 1, EXPERTS_PER_GROUP):
                s = r[a] + r[b]
                top2 = s if top2 is None else jnp.maximum(top2, s)
        if best is None:
            best, gidx = top2, jnp.zeros_like(top2, dtype=jnp.int32)
        else:
            better = top2 > best
            best = jnp.where(better, top2, best)
            gidx = jnp.where(better, g, gidx)
    masked = jnp.where(eidx // EXPERTS_PER_GROUP == gidx, sel, -jnp.inf)
    m1 = jnp.max(masked, axis=0, keepdims=True)
    i1 = jnp.min(jnp.where(masked == m1, eidx, e), axis=0, keepdims=True)
    masked2 = jnp.where(eidx == i1, -jnp.inf, masked)
    m2 = jnp.max(masked2, axis=0, keepdims=True)
    i2 = jnp.min(jnp.where(masked2 == m2, eidx, e), axis=0, keepdims=True)
    hot1 = eidx == i1
    hot2 = eidx == i2
    w1 = jnp.sum(jnp.where(hot1, aff, 0.0), axis=0, keepdims=True)
    w2 = jnp.sum(jnp.where(hot2, aff, 0.0), axis=0, keepdims=True)
    den = w1 + w2
    hot = jnp.where(hot1 | hot2, 1.0, 0.0)
    ra = lax.broadcasted_iota(jnp.int32, (tn, tn), 0)
    rb = lax.broadcasted_iota(jnp.int32, (tn, tn), 1)
    upper = jnp.where(ra <= rb, 1.0, 0.0).astype(BF16)
    incl = jnp.dot(hot.astype(BF16), upper, preferred_element_type=F32)
    pos = run_ref[...] + incl - hot
    rank1 = jnp.sum(jnp.where(hot1, pos, 0.0), axis=0, keepdims=True)
    rank2 = jnp.sum(jnp.where(hot2, pos, 0.0), axis=0, keepdims=True)
    run_ref[...] = run_ref[...] + incl[:, tn - 1:tn]
    idx_ref[...] = jnp.concatenate([i1, i2], axis=0)
    rank_ref[...] = jnp.concatenate([rank1, rank2], axis=0).astype(jnp.int32)
    wrow = jnp.concatenate([w1 / den, w2 / den, jnp.zeros((LANES - TOP_K, tn), F32)], axis=0)
    wt_ref[...] = wrow.T
    cnt_ref[...] = run_ref[...].astype(jnp.int32)


def _route(logits, bias_col, *, tn=512):
    e, m = logits.shape
    tn = _tile(m, tn, LANES)
    row2 = pl.BlockSpec((TOP_K, tn), lambda i: (0, i))
    return pl.pallas_call(
        _route_kernel,
        grid=(m // tn,),
        in_specs=[pl.BlockSpec((e, tn), lambda i: (0, i)), pl.BlockSpec((e, 1), lambda i: (0, 0))],
        out_specs=[row2, row2, pl.BlockSpec((tn, LANES), lambda i: (i, 0)), pl.BlockSpec((e, 1), lambda i: (0, 0))],
        out_shape=[jax.ShapeDtypeStruct((TOP_K, m), jnp.int32), jax.ShapeDtypeStruct((TOP_K, m), jnp.int32),
                   jax.ShapeDtypeStruct((m, LANES), F32), jax.ShapeDtypeStruct((e, 1), jnp.int32)],
        scratch_shapes=[pltpu.VMEM((e, 1), F32)],
        compiler_params=_cparams(("arbitrary",)),
        name="route",
    )(logits, bias_col)


def _moe_plan(counts, n_tokens):
    cnt = counts.reshape(-1)
    gsz = ((cnt + MOE_TM - 1) // MOE_TM) * MOE_TM
    gend = jnp.cumsum(gsz)
    gstart = gend - gsz
    nt = (TOP_K * n_tokens + MOE_TM - 1) // MOE_TM + N_EXPERTS
    used = (gend[-1] // MOE_TM).astype(jnp.int32)
    tile0 = jnp.arange(nt, dtype=jnp.int32) * MOE_TM
    te = jnp.minimum(jnp.sum(gend[None, :] <= tile0[:, None], axis=1), N_EXPERTS - 1).astype(jnp.int32)
    te = jnp.where(jnp.arange(nt) < used, te, te[jnp.maximum(used - 1, 0)])
    pad1 = gend.at[-1].set(nt * MOE_TM)
    return dict(gstart=gstart.astype(jnp.int32).reshape(-1, 1), pad0=(gstart + cnt).astype(jnp.int32),
                pad1=pad1.astype(jnp.int32), tile_expert=te, used=used.reshape(1), n_rows=nt * MOE_TM)


def _dest_kernel(idx_ref, rank_ref, start_ref, dest_ref):
    idx = idx_ref[...]
    out = rank_ref[...]
    for ex in range(N_EXPERTS):
        out = out + jnp.where(idx == ex, start_ref[ex:ex + 1, :], 0)
    dest_ref[...] = out


def _dest(idx, rank, gstart_col, *, tn=512):
    k, m = idx.shape
    tn = _tile(m, tn, LANES)
    row = pl.BlockSpec((k, tn), lambda i: (0, i))
    return pl.pallas_call(
        _dest_kernel,
        grid=(m // tn,),
        in_specs=[row, row, pl.BlockSpec((N_EXPERTS, 1), lambda i: (0, 0))],
        out_specs=row,
        out_shape=jax.ShapeDtypeStruct((k, m), jnp.int32),
        compiler_params=_cparams(("parallel",)),
        name="moe_dest",
    )(idx, rank, gstart_col)


def _dispatch_kernel(d0_ref, d1_ref, pad0_ref, pad1_ref, h_ref, xs_ref, sem, *, m):
    def row_copy(src_row, dst_row):
        return pltpu.make_async_copy(h_ref.at[pl.ds(src_row, 1)], xs_ref.at[pl.ds(dst_row, 1)], sem)

    def wait_rows(n):
        def body(r, carry):
            row_copy(0, 0).wait()
            return carry
        lax.fori_loop(0, n, body, 0)

    def windowed(lo, hi, start_fn, per_step):
        def body(t, carry):
            start_fn(t)

            @pl.when(t - lo >= DMA_WINDOW)
            def _():
                wait_rows(per_step)
            return carry

        lax.fori_loop(lo, hi, body, 0)
        wait_rows(per_step * jnp.minimum(hi - lo, DMA_WINDOW))

    def scatter_token(t):
        row_copy(t, d0_ref[t]).start()
        row_copy(t, d1_ref[t]).start()

    windowed(0, m, scatter_token, TOP_K)
    for ex in range(N_EXPERTS):
        windowed(pad0_ref[ex], pad1_ref[ex], lambda r: row_copy(0, r).start(), 1)


def _dispatch(h, dest0, dest1, pad0, pad1, *, n_rows):
    m, d = h.shape
    return pl.pallas_call(
        functools.partial(_dispatch_kernel, m=m),
        grid_spec=pltpu.PrefetchScalarGridSpec(
            num_scalar_prefetch=4, grid=(1,),
            in_specs=[pl.BlockSpec(memory_space=pl.ANY)],
            out_specs=pl.BlockSpec(memory_space=pl.ANY),
            scratch_shapes=[pltpu.SemaphoreType.DMA(())]),
        out_shape=jax.ShapeDtypeStruct((n_rows, d), h.dtype),
        compiler_params=_cparams(("arbitrary",)),
        name="moe_dispatch",
    )(dest0, dest1, pad0, pad1, h)


def _expert_up_kernel(te_ref, used_ref, x_ref, wg_ref, wu_ref, o_ref):
    active = pl.program_id(1) < used_ref[0]

    @pl.when(active)
    def _():
        x = x_ref[...].astype(BF16)
        g = jnp.dot(x, wg_ref[...], preferred_element_type=F32)
        u = jnp.dot(x, wu_ref[...], preferred_element_type=F32)
        o_ref[...] = (_silu(g) * u).astype(o_ref.dtype)

    @pl.when(jnp.logical_not(active))
    def _():
        o_ref[...] = jnp.zeros_like(o_ref)


def _expert_up(xs, wg, wu, tile_expert, used, *, tn=512):
    r, d = xs.shape
    f = wg.shape[-1]
    tn = _tile(f, tn, LANES)
    wspec = pl.BlockSpec((None, d, tn), lambda j, i, te, us: (te[i], 0, j))
    return pl.pallas_call(
        _expert_up_kernel,
        grid_spec=pltpu.PrefetchScalarGridSpec(
            num_scalar_prefetch=2, grid=(f // tn, r // MOE_TM),
            in_specs=[pl.BlockSpec((MOE_TM, d), lambda j, i, te, us: (i, 0)), wspec, wspec],
            out_specs=pl.BlockSpec((MOE_TM, tn), lambda j, i, te, us: (i, j))),
        out_shape=jax.ShapeDtypeStruct((r, f), BF16),
        compiler_params=_cparams(("arbitrary", "arbitrary")),
        name="moe_up",
    )(tile_expert, used, xs, wg, wu)


def _expert_down_kernel(te_ref, used_ref, a_ref, wd_ref, o_ref):
    active = pl.program_id(1) < used_ref[0]

    @pl.when(active)
    def _():
        o_ref[...] = jnp.dot(a_ref[...], wd_ref[...], preferred_element_type=F32)

    @pl.when(jnp.logical_not(active))
    def _():
        o_ref[...] = jnp.zeros_like(o_ref)


def _expert_down(act, wd, tile_expert, used, *, tn=2048):
    r, f = act.shape
    d = wd.shape[-1]
    tn = _tile(d, tn, LANES)
    return pl.pallas_call(
        _expert_down_kernel,
        grid_spec=pltpu.PrefetchScalarGridSpec(
            num_scalar_prefetch=2, grid=(d // tn, r // MOE_TM),
            in_specs=[pl.BlockSpec((MOE_TM, f), lambda j, i, te, us: (i, 0)),
                      pl.BlockSpec((None, f, tn), lambda j, i, te, us: (te[i], 0, j))],
            out_specs=pl.BlockSpec((MOE_TM, tn), lambda j, i, te, us: (i, j))),
        out_shape=jax.ShapeDtypeStruct((r, d), F32),
        compiler_params=_cparams(("arbitrary", "arbitrary")),
        name="moe_down",
    )(tile_expert, used, act, wd)


def _rope_tables(n_batch, t_lat, t_ctx, width, lane0, scale):
    nf = MLA_ROPE // 4
    pos = jnp.arange(t_lat)
    inv = ROPE_THETA ** (-jnp.arange(nf, dtype=F32) / nf)
    ang_r = (pos // GRID_W).astype(F32)[:, None] * inv
    ang_c = (pos % GRID_W).astype(F32)[:, None] * inv
    z = jnp.zeros((t_lat, nf), F32)
    cos = jnp.concatenate([jnp.cos(ang_r)] * 2 + [jnp.cos(ang_c)] * 2, axis=1)
    s1 = jnp.concatenate([z, jnp.sin(ang_r), z, jnp.sin(ang_c)], axis=1)
    s2 = jnp.concatenate([-jnp.sin(ang_r), z, -jnp.sin(ang_c), z], axis=1)

    def place(rot, ctx_rot, fill):
        def rows(r, n):
            left = jnp.full((n, lane0), fill, F32)
            right = jnp.zeros((n, width - lane0 - MLA_ROPE), F32)
            return jnp.concatenate([left, r, right], axis=1)
        lat = rows(rot, t_lat)
        ctx = rows(jnp.broadcast_to(ctx_rot, (t_ctx, MLA_ROPE)), t_ctx)
        return jnp.concatenate([lat] * n_batch + [ctx] * n_batch, axis=0) * scale

    return (place(cos, jnp.ones((MLA_ROPE,), F32), 1.0), place(s1, jnp.zeros((MLA_ROPE,), F32), 0.0),
            place(s2, jnp.zeros((MLA_ROPE,), F32), 0.0))


def _prep_w_in(w):
    o = {}
    off = 0
    for name, width in (("cq", MLA_Q_RANK), ("ckv", MLA_KV_RANK), ("krope", MLA_ROPE), ("sc", 3 * SC_WIDTH),
                        ("gdn", 2 * GDN_QK + 2 * GDN_VW), ("ab", 2 * N_DIR * GDN_HEADS)):
        o[name] = (off, width)
        off += width
    sl = lambda n: w[:, o[n][0]:o[n][0] + o[n][1]]
    small = jnp.concatenate([sl("cq"), sl("ckv"), sl("krope"), sl("ab")], axis=1).astype(BF16)
    return small, sl("sc").astype(BF16), sl("gdn").astype(BF16), w[:, off:].astype(BF16)


def _prep_w_uq(w):
    r = w.shape[0]
    w3 = w.reshape(r, MLA_HEADS, MLA_QK)
    w3 = jnp.concatenate([w3, jnp.zeros((r, MLA_HEADS, MLA_PAD - MLA_QK), w.dtype)], axis=-1)
    return w3.reshape(r, MLA_HEADS * MLA_PAD).astype(BF16)


def _prep_w_ukv(w):
    r = w.shape[0]
    w3 = w.reshape(r, MLA_HEADS, MLA_NOPE + MLA_V)
    k_top = jnp.concatenate([w3[..., :MLA_NOPE], jnp.zeros((r, MLA_HEADS, MLA_PAD - MLA_NOPE), w.dtype)], -1)
    v_top = w3[..., MLA_NOPE:]
    eye = jnp.eye(MLA_ROPE, dtype=w.dtype)[:, None, :]
    k_rope = jnp.concatenate([jnp.zeros((MLA_ROPE, MLA_HEADS, MLA_NOPE), w.dtype),
                              jnp.broadcast_to(eye, (MLA_ROPE, MLA_HEADS, MLA_ROPE)),
                              jnp.zeros((MLA_ROPE, MLA_HEADS, MLA_PAD - MLA_QK), w.dtype)], -1)
    pad_rows = KV_LHS_W - r - MLA_ROPE
    keys = jnp.concatenate([k_top.reshape(r, -1), k_rope.reshape(MLA_ROPE, -1),
                            jnp.zeros((pad_rows, MLA_HEADS * MLA_PAD), w.dtype)], 0)
    vals = jnp.concatenate([v_top.reshape(r, -1), jnp.zeros((MLA_ROPE + pad_rows, MLA_HEADS * MLA_V), w.dtype)], 0)
    return jnp.concatenate([keys, vals], axis=1).astype(BF16)


def kernel(x, c, ctx, c_ctx, w_ada, b_ada, w_in, q_norm, kv_norm, w_uq, w_ukv, sc_conv, gdn_conv, gdn_a_log,
           gdn_dt_bias, gdn_norm, w_branch, w_o, ln1_g, ln1_b, ln2_g, ln2_b, w_router, router_bias, w_e_gate,
           w_e_up, w_e_down):
    n_batch, t_lat, d = x.shape
    t_ctx = ctx.shape[1]
    depth = w_ada.shape[0]
    n_lat = n_batch * t_lat
    n_all = n_lat + n_batch * t_ctx
    alpha = (2 * depth) ** 0.25
    geo = dict(t_lat=t_lat, n_batch=n_batch)

    xa = jnp.concatenate([x.reshape(n_lat, d), ctx.reshape(n_batch * t_ctx, d)], axis=0)
    cc = jnp.zeros((SUBLANES, d), F32).at[:n_batch].set(c).at[n_batch].set(c_ctx)
    mods = [
        _matmul(cc, w_ada, b_layer=l, out_dtype=F32, tm=SUBLANES, tn=512, lhs_silu=True,
                extra=((b_ada[l][None, :], (1, 512), lambda i, j: (0, j)),), epilogue=_ep_bias,
                name="adaln").reshape(SUBLANES, ADA_MULT, d)
        for l in range(depth)
    ]
    q_tabs = _rope_tables(n_batch, t_lat, t_ctx, MLA_PAD, MLA_NOPE, MLA_QK ** -0.5)
    k_tabs = _rope_tables(n_batch, t_lat, t_ctx, LANES, 0, 1.0)
    w_router_t = w_router.T.astype(BF16)
    bias_col = router_bias.reshape(-1, 1)
    lane_vec = lambda v: jnp.zeros((1, LANES), F32).at[0, 64:96].set(v.reshape(-1))

    h1 = _modulate(xa, mods[0], shift=0, scale=1, **geo)
    for l in range(depth):
        with_ctx = l < depth - 1
        m = n_all if with_ctx else n_lat
        w_small, w_sc, w_gdn, w_gates = _prep_w_in(w_in[l])

        wide = dict(tn=1024, cols_outer=True)
        p_small = _matmul(h1, w_small, out_dtype=F32, tn=SMALL_W, name="in_small")
        p_sc = _matmul(h1, w_sc, out_dtype=F32, m=m, name="in_sc", **wide)
        p_gdn = _matmul(h1, w_gdn, out_dtype=F32, name="in_gdn", **wide)
        gates = _matmul(h1, w_gates, out_dtype=BF16, m=m, epilogue=_ep_sigmoid, name="in_gates", **wide)

        cqn, kvl, gb = _small_prologue(p_small, q_norm[l][None, :], kv_norm[l][None, :], *k_tabs,
                                       lane_vec(-jnp.exp(gdn_a_log[l])), lane_vec(gdn_dt_bias[l]))
        q = _matmul(cqn, _prep_w_uq(w_uq[l]), out_dtype=BF16, m=m, tn=MLA_PAD, epilogue=_ep_rope,
                    extra=tuple((t, (_tile(m, 512, SUBLANES), MLA_PAD), lambda i, j: (i, 0)) for t in q_tabs),
                    name="q_up")
        kv = _matmul(kvl, _prep_w_ukv(w_ukv[l]), out_dtype=BF16, name="kv_up")
        attn = _attention(q, kv, n_batch=n_batch, t_lat=t_lat, t_ctx=t_ctx, latent_queries=True)
        if with_ctx:
            attn_ctx = _attention(q, kv, n_batch=n_batch, t_lat=t_lat, t_ctx=t_ctx, latent_queries=False)
            attn = jnp.concatenate([attn, attn_ctx], axis=0)

        conv = _short_conv(p_sc, sc_conv[l], m=m, n_lat=n_lat, t_lat=t_lat, t_ctx=t_ctx)

        feat = _gdn_features(p_gdn, gdn_conv[l], n_lat=n_lat, t_lat=t_lat, t_ctx=t_ctx)
        o0, o1 = _gdn_scan(feat, gb, _pack_decay_rows(gb), n_batch=n_batch, t_lat=t_lat, t_ctx=t_ctx)
        gdn = _gdn_output(o0, o1, p_gdn, gdn_norm[l][None, :], m=m)

        merged = _merge(attn, conv, gdn, w_branch[l].astype(BF16), gates, m=m)
        mix = _matmul(merged, w_o[l].astype(BF16), out_dtype=F32, name="w_o", **wide)
        x1, h2, logits = _ln_router(xa, mix, mods[l], ln1_g[l][None, :], ln1_b[l][None, :], w_router_t, m=m,
                                    alpha=alpha, **geo)

        idx, rank, wt, counts = _route(logits, bias_col)
        plan = _moe_plan(counts, m)
        dest = _dest(idx, rank, plan["gstart"])
        xs = _dispatch(h2, dest[0], dest[1], plan["pad0"], plan["pad1"], n_rows=plan["n_rows"])
        act = _expert_up(xs, w_e_gate[l].astype(BF16), w_e_up[l].astype(BF16), plan["tile_expert"], plan["used"])
        ys = _expert_down(act, w_e_down[l].astype(BF16), plan["tile_expert"], plan["used"])
        res = _ln_combine(x1, ys, dest[0], dest[1], wt, mods[l], ln2_g[l][None, :], ln2_b[l][None, :], m=m,
                          alpha=alpha, mod_next=mods[l + 1] if with_ctx else None, **geo)
        xa = res[0]
        if with_ctx:
            h1 = res[1]
    return xa[:n_lat].reshape(n_batch, t_lat, d)
```

```python
import functools
import math

import jax
import jax.numpy as jnp
from jax import lax
from jax.experimental import pallas as pl
from jax.experimental.pallas import tpu as pltpu

F32 = jnp.float32
BF16 = jnp.bfloat16

GRID_W = 64
MLA_HEADS = 16
MLA_Q_RANK = 1024
MLA_KV_RANK = 512
MLA_NOPE = 128
MLA_ROPE = 64
MLA_V = 128
MLA_QK = MLA_NOPE + MLA_ROPE
MLA_PAD = 256
ROPE_THETA = 10000.0
SC_WIDTH = 2048
GDN_HEADS = 16
GDN_DK = 128
GDN_DV = 128
GDN_QK = GDN_HEADS * GDN_DK
GDN_VW = GDN_HEADS * GDN_DV
GDN_CHUNK = 64
N_DIR = 2
QUAD = 4
BRANCH_W = 2048
N_BRANCH = 3
ADA_MULT = 6
N_EXPERTS = 16
N_GROUPS = 4
EXPERTS_PER_GROUP = N_EXPERTS // N_GROUPS
TOP_K = 2
EPS = 1e-6
SMALL_W = MLA_Q_RANK + MLA_KV_RANK + 128
KV_LHS_W = MLA_KV_RANK + 128

LANES = 128
SUBLANES = 8
VMEM_LIMIT = 56 * 1024 * 1024
MOE_TM = 256
DMA_WINDOW = 256
HALO = SUBLANES


def _cparams(sem):
    return pltpu.CompilerParams(dimension_semantics=sem, vmem_limit_bytes=VMEM_LIMIT)


def _tile(dim, pref, align):
    if dim <= pref:
        return dim
    t = (pref // align) * align
    while t >= align:
        if dim % t == 0:
            return t
        t -= align
    return dim


def _silu(v):
    return v * jax.nn.sigmoid(v)


def _mm_kernel(*refs, nk, n_extra, epilogue, lhs_silu):
    a_ref, b_ref = refs[0], refs[1]
    extra = refs[2:2 + n_extra]
    o_ref = refs[2 + n_extra]
    a = a_ref[...]
    if lhs_silu:
        a = _silu(a.astype(F32))
    part = jnp.dot(a.astype(BF16), b_ref[...].astype(BF16), preferred_element_type=F32)
    if nk == 1:
        o_ref[...] = epilogue(part, *extra).astype(o_ref.dtype)
        return
    acc_ref = refs[3 + n_extra]
    k = pl.program_id(2)

    @pl.when(k == 0)
    def _():
        acc_ref[...] = part

    @pl.when(k > 0)
    def _():
        acc_ref[...] += part

    @pl.when(k == nk - 1)
    def _():
        o_ref[...] = epilogue(acc_ref[...], *extra).astype(o_ref.dtype)


def _matmul(a, b, *, out_dtype, m=None, tm=512, tn=512, tk=None, b_layer=None, extra=(),
            epilogue=None, lhs_silu=False, cols_outer=False, name="matmul"):
    m = a.shape[0] if m is None else m
    kdim = a.shape[1]
    n = b.shape[-1]
    tm = _tile(m, tm, SUBLANES)
    tn = _tile(n, tn, LANES)
    tk = kdim if tk is None else _tile(kdim, tk, LANES)
    nk = kdim // tk
    if epilogue is None:
        epilogue = lambda acc: acc
    ij = (lambda g0, g1: (g1, g0)) if cols_outer else (lambda g0, g1: (g0, g1))
    grid = (n // tn, m // tm, nk) if cols_outer else (m // tm, n // tn, nk)
    a_spec = pl.BlockSpec((tm, tk), lambda g0, g1, k: (ij(g0, g1)[0], k))
    if b.ndim == 3:
        b_spec = pl.BlockSpec((None, tk, tn), lambda g0, g1, k: (b_layer, k, ij(g0, g1)[1]))
    else:
        b_spec = pl.BlockSpec((tk, tn), lambda g0, g1, k: (k, ij(g0, g1)[1]))
    extra_specs = [pl.BlockSpec(bs, (lambda g0, g1, k, im=im: im(*ij(g0, g1)))) for _, bs, im in extra]
    scratch = [pltpu.VMEM((tm, tn), F32)] if nk > 1 else []
    return pl.pallas_call(
        functools.partial(_mm_kernel, nk=nk, n_extra=len(extra), epilogue=epilogue, lhs_silu=lhs_silu),
        grid=grid,
        in_specs=[a_spec, b_spec] + extra_specs,
        out_specs=pl.BlockSpec((tm, tn), lambda g0, g1, k: ij(g0, g1)),
        out_shape=jax.ShapeDtypeStruct((m, n), out_dtype),
        scratch_shapes=scratch,
        compiler_params=_cparams(("parallel", "parallel", "arbitrary")),
        name=name,
    )(a, b, *[e[0] for e in extra])


def _ep_bias(acc, bias_ref):
    return acc + bias_ref[...]


def _ep_sigmoid(acc):
    return jax.nn.sigmoid(acc)


def _ep_rope(acc, c_ref, s1_ref, s2_ref):
    n = acc.shape[1]
    return (acc * c_ref[...] + pltpu.roll(acc, 16, 1) * s1_ref[...] + pltpu.roll(acc, n - 16, 1) * s2_ref[...])


def _mod_row_map(tm, t_lat, n_batch):
    return lambda i, *_: (jnp.minimum((i * tm) // t_lat, n_batch), 0, 0)


def _modulate_kernel(x_ref, mod_ref, h_ref, *, shift, scale):
    x = x_ref[...]
    h_ref[...] = (x * (1.0 + mod_ref[scale:scale + 1, :]) + mod_ref[shift:shift + 1, :]).astype(h_ref.dtype)


def _modulate(xa, mod, *, shift, scale, t_lat, n_batch, tm=256):
    m, d = xa.shape
    tm = _tile(math.gcd(m, t_lat), tm, SUBLANES)
    return pl.pallas_call(
        functools.partial(_modulate_kernel, shift=shift, scale=scale),
        grid=(m // tm,),
        in_specs=[pl.BlockSpec((tm, d), lambda i: (i, 0)),
                  pl.BlockSpec((None, ADA_MULT, d), _mod_row_map(tm, t_lat, n_batch))],
        out_specs=pl.BlockSpec((tm, d), lambda i: (i, 0)),
        out_shape=jax.ShapeDtypeStruct((m, d), BF16),
        compiler_params=_cparams(("parallel",)),
        name="modulate",
    )(xa, mod)


def _post_norm(x, y, gate_row, lnw_ref, lnb_ref, alpha):
    v = alpha * x + gate_row * y
    vc = v - jnp.mean(v, axis=-1, keepdims=True)
    var = jnp.mean(vc * vc, axis=-1, keepdims=True)
    return vc * lax.rsqrt(var + EPS) * lnw_ref[...] + lnb_ref[...]


def _ln_router_kernel(x_ref, y_ref, mod_ref, lnw_ref, lnb_ref, wr_ref, xo_ref, h_ref, lg_ref, *, alpha):
    xn = _post_norm(x_ref[...], y_ref[...], mod_ref[2:3, :], lnw_ref, lnb_ref, alpha)
    xo_ref[...] = xn
    h = xn * (1.0 + mod_ref[4:5, :]) + mod_ref[3:4, :]
    h_ref[...] = h
    lg_ref[...] = lax.dot_general(wr_ref[...], h.astype(BF16), (((1,), (1,)), ((), ())),
                                  preferred_element_type=F32)


def _ln_router(xa, y, mod, lnw, lnb, w_router_t, *, m, alpha, t_lat, n_batch, tm=256):
    d = xa.shape[1]
    e = w_router_t.shape[0]
    tm = _tile(math.gcd(m, t_lat), tm, LANES)
    row = lambda i: (i, 0)
    const = lambda i: (0, 0)
    return pl.pallas_call(
        functools.partial(_ln_router_kernel, alpha=alpha),
        grid=(m // tm,),
        in_specs=[pl.BlockSpec((tm, d), row), pl.BlockSpec((tm, d), row),
                  pl.BlockSpec((None, ADA_MULT, d), _mod_row_map(tm, t_lat, n_batch)),
                  pl.BlockSpec((1, d), const), pl.BlockSpec((1, d), const), pl.BlockSpec((e, d), const)],
        out_specs=[pl.BlockSpec((tm, d), row), pl.BlockSpec((tm, d), row), pl.BlockSpec((e, tm), lambda i: (0, i))],
        out_shape=[jax.ShapeDtypeStruct((m, d), F32), jax.ShapeDtypeStruct((m, d), F32),
                   jax.ShapeDtypeStruct((e, m), F32)],
        compiler_params=_cparams(("parallel",)),
        name="ln_router",
    )(xa, y, mod, lnw, lnb, w_router_t)


def _ln_combine_kernel(d0_ref, d1_ref, x_ref, y_ref, wt_ref, mod_ref, lnw_ref, lnb_ref, *rest, alpha, tm,
                       with_h):
    if with_h:
        modn_ref, xo_ref, h_ref, buf, sem = rest
    else:
        xo_ref, buf, sem = rest
    row0 = pl.program_id(0) * tm

    def row_copy(src_row, slot, r):
        return pltpu.make_async_copy(y_ref.at[pl.ds(src_row, 1)], buf.at[slot, pl.ds(r, 1)], sem)

    def issue(r, carry):
        row_copy(d0_ref[row0 + r], 0, r).start()
        row_copy(d1_ref[row0 + r], 1, r).start()
        return carry

    lax.fori_loop(0, tm, issue, 0)

    def wait(r, carry):
        row_copy(0, 0, 0).wait()
        row_copy(0, 1, 0).wait()
        return carry

    lax.fori_loop(0, tm, wait, 0)
    ffn = wt_ref[:, 0:1] * buf[0] + wt_ref[:, 1:2] * buf[1]
    xn = _post_norm(x_ref[...], ffn, mod_ref[5:6, :], lnw_ref, lnb_ref, alpha)
    xo_ref[...] = xn
    if with_h:
        h_ref[...] = (xn * (1.0 + modn_ref[1:2, :]) + modn_ref[0:1, :]).astype(h_ref.dtype)


def _ln_combine(x1, y_sorted, dest0, dest1, wt, mod, lnw, lnb, *, m, alpha, t_lat, n_batch, mod_next=None,
                tm=256):
    d = x1.shape[1]
    tm = _tile(math.gcd(m, t_lat), tm, SUBLANES)
    with_h = mod_next is not None
    row = lambda i, *_: (i, 0)
    const = lambda i, *_: (0, 0)
    modmap = _mod_row_map(tm, t_lat, n_batch)
    in_specs = [pl.BlockSpec((tm, d), row), pl.BlockSpec(memory_space=pl.ANY), pl.BlockSpec((tm, LANES), row),
                pl.BlockSpec((None, ADA_MULT, d), modmap), pl.BlockSpec((1, d), const), pl.BlockSpec((1, d), const)]
    args = [x1, y_sorted, wt, mod, lnw, lnb]
    out_specs = [pl.BlockSpec((tm, d), row)]
    out_shape = [jax.ShapeDtypeStruct((m, d), F32)]
    if with_h:
        in_specs.append(pl.BlockSpec((None, ADA_MULT, d), modmap))
        args.append(mod_next)
        out_specs.append(pl.BlockSpec((tm, d), row))
        out_shape.append(jax.ShapeDtypeStruct((m, d), BF16))
    return pl.pallas_call(
        functools.partial(_ln_combine_kernel, alpha=alpha, tm=tm, with_h=with_h),
        grid_spec=pltpu.PrefetchScalarGridSpec(
            num_scalar_prefetch=2, grid=(m // tm,), in_specs=in_specs, out_specs=out_specs,
            scratch_shapes=[pltpu.VMEM((TOP_K, tm, d), F32), pltpu.SemaphoreType.DMA(())]),
        out_shape=out_shape,
        compiler_params=_cparams(("arbitrary",)),
        name="ln_combine",
    )(dest0, dest1, *args)


def _seq_edges(i, tm, n_lat, t_lat, t_ctx):
    row0 = i * tm
    in_lat = row0 < n_lat
    start = jnp.where(in_lat, row0 % t_lat == 0, (row0 - n_lat) % t_ctx == 0)
    end = jnp.where(in_lat, (row0 + tm) % t_lat == 0, (row0 + tm - n_lat) % t_ctx == 0)
    return start, end


def _conv3(u, prev_row, next_row, w_ref):
    tm = u.shape[0]
    rows = lax.broadcasted_iota(jnp.int32, u.shape, 0)
    um1 = jnp.where(rows == 0, prev_row, pltpu.roll(u, 1, 0))
    up1 = jnp.where(rows == tm - 1, next_row, pltpu.roll(u, tm - 1, 0))
    return w_ref[0:1, :] * um1 + w_ref[1:2, :] * u + w_ref[2:3, :] * up1


def _halo_specs(tm, tc, col_of_j, n_rows):
    last = n_rows // HALO - 1
    prev = pl.BlockSpec((HALO, tc), lambda i, j: (jnp.maximum(i * (tm // HALO) - 1, 0), col_of_j(j)))
    nxt = pl.BlockSpec((HALO, tc), lambda i, j: (jnp.minimum((i + 1) * (tm // HALO), last), col_of_j(j)))
    return prev, nxt


def _sconv_kernel(b_ref, c_ref, x_ref, cp_ref, xp_ref, cn_ref, xn_ref, w_ref, o_ref, *, tm, n_lat, t_lat,
                  t_ctx):
    start, end = _seq_edges(pl.program_id(0), tm, n_lat, t_lat, t_ctx)
    u = c_ref[...] * x_ref[...]
    prev = jnp.where(start, 0.0, cp_ref[HALO - 1:HALO, :] * xp_ref[HALO - 1:HALO, :])
    nxt = jnp.where(end, 0.0, cn_ref[0:1, :] * xn_ref[0:1, :])
    o_ref[...] = (b_ref[...] * _conv3(u, prev, nxt, w_ref)).astype(o_ref.dtype)


def _short_conv(p_sc, w, *, m, n_lat, t_lat, t_ctx, tm=256, tc=512):
    n_rows = p_sc.shape[0]
    width = SC_WIDTH
    tc = _tile(width, tc, LANES)
    tm = _tile(math.gcd(t_lat, t_ctx), tm, HALO)
    nb = width // tc
    blk = lambda g: pl.BlockSpec((tm, tc), lambda i, j: (i, g * nb + j))
    cp, cn = _halo_specs(tm, tc, lambda j: nb + j, n_rows)
    xp, xn = _halo_specs(tm, tc, lambda j: 2 * nb + j, n_rows)
    return pl.pallas_call(
        functools.partial(_sconv_kernel, tm=tm, n_lat=n_lat, t_lat=t_lat, t_ctx=t_ctx),
        grid=(m // tm, nb),
        in_specs=[blk(0), blk(1), blk(2), cp, xp, cn, xn, pl.BlockSpec((3, tc), lambda i, j: (0, j))],
        out_specs=pl.BlockSpec((tm, tc), lambda i, j: (i, j)),
        out_shape=jax.ShapeDtypeStruct((m, width), BF16),
        compiler_params=_cparams(("parallel", "parallel")),
        name="short_conv",
    )(p_sc, p_sc, p_sc, p_sc, p_sc, p_sc, p_sc, w)


def _small_kernel(p_ref, qn_ref, kvn_ref, ck_ref, s1_ref, s2_ref, nexp_ref, dtb_ref, cq_ref, kv_ref, gb_ref):
    cq = p_ref[:, :MLA_Q_RANK]
    cq_ref[...] = (cq * lax.rsqrt(jnp.mean(cq * cq, axis=-1, keepdims=True) + EPS) * qn_ref[...]).astype(
        cq_ref.dtype)
    ckv = p_ref[:, MLA_Q_RANK:MLA_Q_RANK + MLA_KV_RANK]
    kv_ref[:, :MLA_KV_RANK] = (ckv * lax.rsqrt(jnp.mean(ckv * ckv, axis=-1, keepdims=True) + EPS)
                               * kvn_ref[...]).astype(kv_ref.dtype)
    x = p_ref[:, MLA_Q_RANK + MLA_KV_RANK:]
    rot = x * ck_ref[...] + pltpu.roll(x, 16, 1) * s1_ref[...] + pltpu.roll(x, LANES - 16, 1) * s2_ref[...]
    kv_ref[:, MLA_KV_RANK:] = rot.astype(kv_ref.dtype)
    z = x + dtb_ref[...]
    g = nexp_ref[...] * (jnp.maximum(z, 0.0) + jnp.log1p(jnp.exp(-jnp.abs(z))))
    tm = x.shape[0]
    rows = lax.broadcasted_iota(jnp.int32, x.shape, 0) % GDN_CHUNK
    lanes = lax.broadcasted_iota(jnp.int32, x.shape, 1)
    fwd = g
    rev = g
    s = 1
    while s < GDN_CHUNK:
        fwd = fwd + jnp.where(rows >= s, pltpu.roll(fwd, s, 0), 0.0)
        rev = rev + jnp.where(rows < GDN_CHUNK - s, pltpu.roll(rev, tm - s, 0), 0.0)
        s *= 2
    beta = jax.nn.sigmoid(x)
    gb_ref[...] = jnp.where(lanes < 64, 0.0, jnp.where(lanes < 80, fwd, jnp.where(lanes < 96, rev, beta)))


def _small_prologue(p_small, q_norm, kv_norm, ck, s1, s2, nexp, dtb, *, tm=256):
    m = p_small.shape[0]
    tm = _tile(m, tm, GDN_CHUNK)
    row = lambda i: (i, 0)
    vec = lambda w: pl.BlockSpec((1, w), lambda i: (0, 0))
    tab = pl.BlockSpec((tm, LANES), row)
    return pl.pallas_call(
        _small_kernel,
        grid=(m // tm,),
        in_specs=[pl.BlockSpec((tm, SMALL_W), row), vec(MLA_Q_RANK), vec(MLA_KV_RANK), tab, tab, tab,
                  vec(LANES), vec(LANES)],
        out_specs=[pl.BlockSpec((tm, MLA_Q_RANK), row), pl.BlockSpec((tm, KV_LHS_W), row),
                   pl.BlockSpec((tm, LANES), row)],
        out_shape=[jax.ShapeDtypeStruct((m, MLA_Q_RANK), BF16), jax.ShapeDtypeStruct((m, KV_LHS_W), BF16),
                   jax.ShapeDtypeStruct((m, LANES), F32)],
        compiler_params=_cparams(("parallel",)),
        name="small_prologue",
    )(p_small, q_norm, kv_norm, ck, s1, s2, nexp, dtb)


def _attn_kernel(*refs, with_lat):
    if with_lat:
        q_ref, kc_ref, vc_ref, kl_ref, vl_ref, o_ref = refs
    else:
        q_ref, kc_ref, vc_ref, o_ref = refs
    nt = (((1,), (1,)), ((), ()))
    q = q_ref[...]
    sc = lax.dot_general(q, kc_ref[...], nt, preferred_element_type=F32)
    mx = jnp.max(sc, axis=-1, keepdims=True)
    if with_lat:
        sl = lax.dot_general(q, kl_ref[...], nt, preferred_element_type=F32)
        mx = jnp.maximum(mx, jnp.max(sl, axis=-1, keepdims=True))
    pc = jnp.exp(sc - mx)
    den = jnp.sum(pc, axis=-1, keepdims=True)
    acc = jnp.dot(pc.astype(BF16), vc_ref[...], preferred_element_type=F32)
    if with_lat:
        pl_ = jnp.exp(sl - mx)
        den = den + jnp.sum(pl_, axis=-1, keepdims=True)
        acc = acc + jnp.dot(pl_.astype(BF16), vl_ref[...], preferred_element_type=F32)
    o_ref[...] = (acc / den).astype(o_ref.dtype)


def _attention(q, kv, *, n_batch, t_lat, t_ctx, latent_queries, tq=256):
    n_lat = n_batch * t_lat
    h = MLA_HEADS
    vcol0 = h * MLA_PAD // MLA_V
    if latent_queries:
        tq = _tile(t_lat, tq, SUBLANES)
        nq = t_lat // tq
        m_out = n_lat
        qrow = lambda b, hh, qi: (b * nq + qi, hh)
        orow = qrow
    else:
        tq = t_ctx
        nq = 1
        m_out = n_batch * t_ctx
        qrow = lambda b, hh, qi: (n_lat // t_ctx + b, hh)
        orow = lambda b, hh, qi: (b, hh)
    ctx_blk = n_lat // t_ctx
    in_specs = [pl.BlockSpec((tq, MLA_PAD), qrow),
                pl.BlockSpec((t_ctx, MLA_PAD), lambda b, hh, qi: (ctx_blk + b, hh)),
                pl.BlockSpec((t_ctx, MLA_V), lambda b, hh, qi: (ctx_blk + b, vcol0 + hh))]
    args = [q, kv, kv]
    if latent_queries:
        in_specs += [pl.BlockSpec((t_lat, MLA_PAD), lambda b, hh, qi: (b, hh)),
                     pl.BlockSpec((t_lat, MLA_V), lambda b, hh, qi: (b, vcol0 + hh))]
        args += [kv, kv]
    return pl.pallas_call(
        functools.partial(_attn_kernel, with_lat=latent_queries),
        grid=(n_batch, h, nq),
        in_specs=in_specs,
        out_specs=pl.BlockSpec((tq, MLA_V), orow),
        out_shape=jax.ShapeDtypeStruct((m_out, h * MLA_V), BF16),
        compiler_params=_cparams(("parallel", "parallel", "arbitrary")),
        name="attention_lat" if latent_queries else "attention_ctx",
    )(*args)


def _gdn_feat_kernel(x_ref, xp_ref, xn_ref, w_ref, o_ref, *, tm, tc, n_lat, t_lat, t_ctx):
    start, end = _seq_edges(pl.program_id(0), tm, n_lat, t_lat, t_ctx)
    j = pl.program_id(1)
    prev = jnp.where(start, 0.0, xp_ref[HALO - 1:HALO, :])
    nxt = jnp.where(end, 0.0, xn_ref[0:1, :])
    f = _silu(_conv3(x_ref[...], prev, nxt, w_ref))
    is_q = j < GDN_QK // tc
    is_v = j >= 2 * GDN_QK // tc
    post = jnp.where(is_q, GDN_DK ** -0.5, 1.0)
    for hh in range(tc // GDN_DK):
        fh = f[:, hh * GDN_DK:(hh + 1) * GDN_DK]
        inv = lax.rsqrt(jnp.sum(fh * fh, axis=-1, keepdims=True) + EPS) * post
        o_ref[:, hh * GDN_DK:(hh + 1) * GDN_DK] = fh * jnp.where(is_v, 1.0, inv)


def _gdn_features(p_gdn, w, *, n_lat, t_lat, t_ctx, tm=256, tc=512):
    n_rows = p_gdn.shape[0]
    width = 2 * GDN_QK + GDN_VW
    tc = _tile(GDN_QK, tc, GDN_DK)
    tm = _tile(math.gcd(t_lat, t_ctx), tm, HALO)
    xp, xn = _halo_specs(tm, tc, lambda j: j, n_rows)
    return pl.pallas_call(
        functools.partial(_gdn_feat_kernel, tm=tm, tc=tc, n_lat=n_lat, t_lat=t_lat, t_ctx=t_ctx),
        grid=(n_rows // tm, width // tc),
        in_specs=[pl.BlockSpec((tm, tc), lambda i, j: (i, j)), xp, xn,
                  pl.BlockSpec((3, tc), lambda i, j: (0, j))],
        out_specs=pl.BlockSpec((tm, tc), lambda i, j: (i, j)),
        out_shape=jax.ShapeDtypeStruct((n_rows, width), F32),
        compiler_params=_cparams(("parallel", "parallel")),
        name="gdn_features",
    )(p_gdn, p_gdn, p_gdn, w)


def _block_diag(xp, blk, n):
    return jnp.concatenate([jnp.where(blk == p, xp, jnp.zeros_like(xp)) for p in range(n)], axis=0)


def _gdn_scan_kernel(fq0, fk0, fv0, gb0, gr0, fq1, fk1, fv1, gb1, gr1, o0_ref, o1_ref, state_ref):
    c = GDN_CHUNK
    pw = QUAD * c
    nt = (((1,), (1,)), ((), ()))
    tn = (((0,), (0,)), ((), ()))

    @pl.when(pl.program_id(1) == 0)
    def _():
        state_ref[...] = jnp.zeros_like(state_ref)

    ri = lax.broadcasted_iota(jnp.int32, (c, pw), 0)
    ci = lax.broadcasted_iota(jnp.int32, (c, pw), 1)
    cj = ci % c
    blk = ci // c
    blk_k = lax.broadcasted_iota(jnp.int32, (c, QUAD * GDN_DK), 1) // GDN_DK
    eye_p = jnp.where(ri == cj, 1.0, 0.0)
    lanes = lax.broadcasted_iota(jnp.int32, (c, LANES), 1)
    dirs = ((fq0, fk0, fv0, gb0, gr0), (fq1, fk1, fv1, gb1, gr1))
    nq = GDN_HEADS // QUAD

    quads = []
    for d, (fq, fk, fv, gb_ref, gr_ref) in enumerate(dirs):
        incl = (ri <= cj) if d == 1 else (ri >= cj)
        strict = (ri < cj) if d == 1 else (ri > cj)
        gbv = gb_ref[...]
        for g in range(nq):
            cols = slice(g * QUAD * GDN_DK, (g + 1) * QUAD * GDN_DK)
            k4, q4, v4 = fk[:, cols], fq[:, cols], fv[:, cols]
            grow = gr_ref[d * nq + g:d * nq + g + 1, :]
            heads, kb_l, gcol_p = [], [], None
            for p in range(QUAD):
                h = g * QUAD + p
                hs = slice(p * GDN_DK, (p + 1) * GDN_DK)
                gcol = jnp.sum(jnp.where(lanes == 64 + d * GDN_HEADS + h, gbv, 0.0), axis=1, keepdims=True)
                bcol = jnp.sum(jnp.where(lanes == 96 + d * GDN_HEADS + h, gbv, 0.0), axis=1, keepdims=True)
                gend = grow[:, p * c:p * c + 1] if d == 1 else grow[:, (p + 1) * c - 1:(p + 1) * c]
                egc = jnp.exp(gcol)
                kp = k4[:, hs]
                kb = kp * bcol
                kb_l.append(kb)
                heads.append(dict(
                    h=h,
                    rhs=jnp.concatenate([v4[:, hs] * bcol, kb * egc], axis=1).astype(BF16),
                    qe=q4[:, hs] * egc,
                    kt=(kp * jnp.exp(gend - gcol)).astype(BF16),
                    cd=jnp.exp(gend),
                ))
                gcol_p = gcol if gcol_p is None else jnp.where(blk == p, gcol, gcol_p)
            decay = jnp.exp(jnp.where(incl, gcol_p - grow, -jnp.inf))
            kbd = _block_diag(k4.astype(BF16), blk_k, QUAD)
            lhs = jnp.concatenate([jnp.concatenate(kb_l, axis=1), q4], axis=0).astype(BF16)
            quads.append(dict(d=d, heads=heads, decay=decay, strict=strict, kbd=kbd, lhs=lhs))

    for qd in quads:
        sc = lax.dot_general(qd["lhs"], qd["kbd"], nt, preferred_element_type=F32)
        qd["x"] = -jnp.where(qd["strict"], sc[:c] * qd["decay"], 0.0)
        qd["intra"] = (sc[c:] * qd["decay"]).astype(BF16)
        qd["tinv"] = eye_p + qd["x"]

    def split(v):
        hi = v.astype(BF16)
        return hi, (v - hi.astype(F32)).astype(BF16)

    def times_bd(v, bd_hi, bd_lo):
        hi, lo = split(v)
        r = jnp.dot(jnp.concatenate([hi, lo], axis=0), bd_hi, preferred_element_type=F32)
        return r[:c] + r[c:] + jnp.dot(hi, bd_lo, preferred_element_type=F32)

    def bd_split(v):
        hi, lo = split(v)
        return _block_diag(hi, blk, QUAD), _block_diag(lo, blk, QUAD)

    for qd in quads:
        qd["bd"] = bd_split(qd["x"])
    step = 2
    while step < c:
        for qd in quads:
            qd["x"] = times_bd(qd["x"], *qd["bd"])
        for qd in quads:
            qd["bd"] = bd_split(qd["x"])
        for qd in quads:
            qd["tinv"] = qd["tinv"] + times_bd(qd["tinv"], *qd["bd"])
        step *= 2

    for qd in quads:
        t16 = qd["tinv"].astype(BF16)
        for p, hd in enumerate(qd["heads"]):
            hd["uw"] = jnp.dot(t16[:, p * c:(p + 1) * c], hd["rhs"], preferred_element_type=F32)

    for qd in quads:
        for hd in qd["heads"]:
            hd["st"] = state_ref[qd["d"], hd["h"]]
            lhs = jnp.concatenate([hd["uw"][:, GDN_DV:], hd["qe"]], axis=0).astype(BF16)
            hd["ws"] = jnp.dot(lhs, hd["st"].astype(BF16), preferred_element_type=F32)
    for qd in quads:
        for hd in qd["heads"]:
            hd["vn"] = (hd["uw"][:, :GDN_DV] - hd["ws"][:c]).astype(BF16)
    outs = ([], [])
    for qd in quads:
        for p, hd in enumerate(qd["heads"]):
            outs[qd["d"]].append(hd["ws"][c:] + jnp.dot(qd["intra"][:, p * c:(p + 1) * c], hd["vn"],
                                                         preferred_element_type=F32))
            state_ref[qd["d"], hd["h"]] = hd["st"] * hd["cd"] + lax.dot_general(
                hd["kt"], hd["vn"], tn, preferred_element_type=F32)
    o0_ref[...] = jnp.concatenate(outs[0], axis=1)
    o1_ref[...] = jnp.concatenate(outs[1], axis=1)


def _gdn_scan(feat, gb, grows, *, n_batch, t_lat, t_ctx):
    n_rows = feat.shape[0]
    c = GDN_CHUNK
    n_lat = n_batch * t_lat
    nc_ctx, nc_lat = t_ctx // c, t_lat // c

    def chunk_row(d):
        def f(b, s):
            if d == 0:
                return jnp.where(s < nc_ctx, n_lat // c + b * nc_ctx + s, b * nc_lat + (s - nc_ctx))
            return jnp.where(s < nc_ctx, n_lat // c + b * nc_ctx + (nc_ctx - 1 - s),
                             b * nc_lat + (nc_lat - 1 - (s - nc_ctx)))
        return f

    in_specs, args = [], []
    for d in range(N_DIR):
        cr = chunk_row(d)
        for col0 in range(3):
            in_specs.append(pl.BlockSpec((c, GDN_QK), lambda b, s, cr=cr, col0=col0: (cr(b, s), col0)))
            args.append(feat)
        in_specs.append(pl.BlockSpec((c, LANES), lambda b, s, cr=cr: (cr(b, s), 0)))
        args.append(gb)
        in_specs.append(pl.BlockSpec((None,) + grows.shape[1:], lambda b, s, cr=cr: (cr(b, s), 0, 0)))
        args.append(grows)
    out_specs = [pl.BlockSpec((c, GDN_VW), lambda b, s, cr=chunk_row(d): (cr(b, s), 0)) for d in range(N_DIR)]
    return pl.pallas_call(
        _gdn_scan_kernel,
        grid=(n_batch, nc_ctx + nc_lat),
        in_specs=in_specs, out_specs=out_specs,
        out_shape=[jax.ShapeDtypeStruct((n_rows, GDN_VW), F32)] * N_DIR,
        scratch_shapes=[pltpu.VMEM((N_DIR, GDN_HEADS, GDN_DK, GDN_DV), F32)],
        compiler_params=_cparams(("parallel", "arbitrary")),
        name="gdn_scan",
    )(*args)


def _pack_decay_rows(gb):
    n_chunks = gb.shape[0] // GDN_CHUNK
    g = gb[:, 64:96].reshape(n_chunks, GDN_CHUNK, N_DIR, GDN_HEADS // QUAD, QUAD)
    return jnp.transpose(g, (0, 2, 3, 4, 1)).reshape(n_chunks, N_DIR * GDN_HEADS // QUAD, QUAD * GDN_CHUNK)


def _gdn_out_kernel(o0_ref, o1_ref, z_ref, w_ref, y_ref):
    o = o0_ref[...] + o1_ref[...]
    z = z_ref[...]
    for hh in range(o.shape[1] // GDN_DV):
        sl = slice(hh * GDN_DV, (hh + 1) * GDN_DV)
        oh = o[:, sl]
        yh = oh * lax.rsqrt(jnp.mean(oh * oh, axis=-1, keepdims=True) + EPS) * w_ref[...]
        y_ref[:, sl] = (yh * _silu(z[:, sl])).astype(y_ref.dtype)


def _gdn_output(o0, o1, p_gdn, w, *, m, tm=256, tc=512):
    tc = _tile(GDN_VW, tc, GDN_DV)
    tm = _tile(m, tm, 16)
    z0 = (2 * GDN_QK + GDN_VW) // tc
    blk = pl.BlockSpec((tm, tc), lambda i, j: (i, j))
    return pl.pallas_call(
        _gdn_out_kernel,
        grid=(m // tm, GDN_VW // tc),
        in_specs=[blk, blk, pl.BlockSpec((tm, tc), lambda i, j: (i, z0 + j)),
                  pl.BlockSpec((1, GDN_DV), lambda i, j: (0, 0))],
        out_specs=blk,
        out_shape=jax.ShapeDtypeStruct((m, GDN_VW), BF16),
        compiler_params=_cparams(("parallel", "parallel")),
        name="gdn_output",
    )(o0, o1, p_gdn, w)


def _merge_kernel(a0, a1, a2, w_ref, g0, g1, g2, o_ref):
    acc = g0[...].astype(F32) * jnp.dot(a0[...], w_ref[0], preferred_element_type=F32)
    acc += g1[...].astype(F32) * jnp.dot(a1[...], w_ref[1], preferred_element_type=F32)
    acc += g2[...].astype(F32) * jnp.dot(a2[...], w_ref[2], preferred_element_type=F32)
    o_ref[...] = acc.astype(o_ref.dtype)


def _merge(attn, conv, gdn, w_branch, gates, *, m, tm=512, tn=512):
    d = w_branch.shape[-1]
    tm = _tile(m, tm, 16)
    tn = _tile(d, tn, LANES)
    nb = d // tn
    lhs = pl.BlockSpec((tm, BRANCH_W), lambda j, i: (i, 0))
    gate = lambda n: pl.BlockSpec((tm, tn), lambda j, i: (i, n * nb + j))
    return pl.pallas_call(
        _merge_kernel,
        grid=(nb, m // tm),
        in_specs=[lhs, lhs, lhs, pl.BlockSpec((N_BRANCH, BRANCH_W, tn), lambda j, i: (0, 0, j)),
                  gate(0), gate(1), gate(2)],
        out_specs=pl.BlockSpec((tm, tn), lambda j, i: (i, j)),
        out_shape=jax.ShapeDtypeStruct((m, d), BF16),
        compiler_params=_cparams(("parallel", "parallel")),
        name="merge",
    )(attn, conv, gdn, w_branch, gates, gates, gates)


def _route_kernel(lg_ref, bias_ref, idx_ref, rank_ref, wt_ref, cnt_ref, run_ref):
    @pl.when(pl.program_id(0) == 0)
    def _():
        run_ref[...] = jnp.zeros_like(run_ref)

    aff = jax.nn.sigmoid(lg_ref[...])
    sel = aff + bias_ref[...]
    e, tn = aff.shape
    eidx = lax.broadcasted_iota(jnp.int32, (e, tn), 0)
    best = None
    for g in range(N_GROUPS):
        r = [sel[g * EXPERTS_PER_GROUP + t:g * EXPERTS_PER_GROUP + t + 1, :] for t in range(EXPERTS_PER_GROUP)]
        top2 = None
        for a in range(EXPERTS_PER_GROUP):
            for b in range(a + 1, EXPERTS_PER_GROUP):
                s = r[a] + r[b]
                top2 = s if top2 is None else jnp.maximum(top2, s)
        if best is None:
            best, gidx = top2, jnp.zeros_like(top2, dtype=jnp.int32)
        else:
            better = top2 > best
            best = jnp.where(better, top2, best)
            gidx = jnp.where(better, g, gidx)
    masked = jnp.where(eidx // EXPERTS_PER_GROUP == gidx, sel, -jnp.inf)
    m1 = jnp.max(masked, axis=0, keepdims=True)
    i1 = jnp.min(jnp.where(masked == m1, eidx, e), axis=0, keepdims=True)
    masked2 = jnp.where(eidx == i1, -jnp.inf, masked)
    m2 = jnp.max(masked2, axis=0, keepdims=True)
    i2 = jnp.min(jnp.where(masked2 == m2, eidx, e), axis=0, keepdims=True)
    hot1 = eidx == i1
    hot2 = eidx == i2
    w1 = jnp.sum(jnp.where(hot1, aff, 0.0), axis=0, keepdims=True)
    w2 = jnp.sum(jnp.where(hot2, aff, 0.0), axis=0, keepdims=True)
    den = w1 + w2
    hot = jnp.where(hot1 | hot2, 1.0, 0.0)
    ra = lax.broadcasted_iota(jnp.int32, (tn, tn), 0)
    rb = lax.broadcasted_iota(jnp.int32, (tn, tn), 1)
    upper = jnp.where(ra <= rb, 1.0, 0.0).astype(BF16)
    incl = jnp.dot(hot.astype(BF16), upper, preferred_element_type=F32)
    pos = run_ref[...] + incl - hot
    rank1 = jnp.sum(jnp.where(hot1, pos, 0.0), axis=0, keepdims=True)
    rank2 = jnp.sum(jnp.where(hot2, pos, 0.0), axis=0, keepdims=True)
    run_ref[...] = run_ref[...] + incl[:, tn - 1:tn]
    idx_ref[...] = jnp.concatenate([i1, i2], axis=0)
    rank_ref[...] = jnp.concatenate([rank1, rank2], axis=0).astype(jnp.int32)
    wrow = jnp.concatenate([w1 / den, w2 / den, jnp.zeros((LANES - TOP_K, tn), F32)], axis=0)
    wt_ref[...] = wrow.T
    cnt_ref[...] = run_ref[...].astype(jnp.int32)


def _route(logits, bias_col, *, tn=512):
    e, m = logits.shape
    tn = _tile(m, tn, LANES)
    row2 = pl.BlockSpec((TOP_K, tn), lambda i: (0, i))
    return pl.pallas_call(
        _route_kernel,
        grid=(m // tn,),
        in_specs=[pl.BlockSpec((e, tn), lambda i: (0, i)), pl.BlockSpec((e, 1), lambda i: (0, 0))],
        out_specs=[row2, row2, pl.BlockSpec((tn, LANES), lambda i: (i, 0)), pl.BlockSpec((e, 1), lambda i: (0, 0))],
        out_shape=[jax.ShapeDtypeStruct((TOP_K, m), jnp.int32), jax.ShapeDtypeStruct((TOP_K, m), jnp.int32),
                   jax.ShapeDtypeStruct((m, LANES), F32), jax.ShapeDtypeStruct((e, 1), jnp.int32)],
        scratch_shapes=[pltpu.VMEM((e, 1), F32)],
        compiler_params=_cparams(("arbitrary",)),
        name="route",
    )(logits, bias_col)


def _moe_plan(counts, n_tokens):
    cnt = counts.reshape(-1)
    gsz = ((cnt + MOE_TM - 1) // MOE_TM) * MOE_TM
    gend = jnp.cumsum(gsz)
    gstart = gend - gsz
    nt = (TOP_K * n_tokens + MOE_TM - 1) // MOE_TM + N_EXPERTS
    used = (gend[-1] // MOE_TM).astype(jnp.int32)
    tile0 = jnp.arange(nt, dtype=jnp.int32) * MOE_TM
    te = jnp.minimum(jnp.sum(gend[None, :] <= tile0[:, None], axis=1), N_EXPERTS - 1).astype(jnp.int32)
    te = jnp.where(jnp.arange(nt) < used, te, te[jnp.maximum(used - 1, 0)])
    pad1 = gend.at[-1].set(nt * MOE_TM)
    return dict(gstart=gstart.astype(jnp.int32).reshape(-1, 1), pad0=(gstart + cnt).astype(jnp.int32),
                pad1=pad1.astype(jnp.int32), tile_expert=te, used=used.reshape(1), n_rows=nt * MOE_TM)


def _dest_kernel(idx_ref, rank_ref, start_ref, dest_ref):
    idx = idx_ref[...]
    out = rank_ref[...]
    for ex in range(N_EXPERTS):
        out = out + jnp.where(idx == ex, start_ref[ex:ex + 1, :], 0)
    dest_ref[...] = out


def _dest(idx, rank, gstart_col, *, tn=512):
    k, m = idx.shape
    tn = _tile(m, tn, LANES)
    row = pl.BlockSpec((k, tn), lambda i: (0, i))
    return pl.pallas_call(
        _dest_kernel,
        grid=(m // tn,),
        in_specs=[row, row, pl.BlockSpec((N_EXPERTS, 1), lambda i: (0, 0))],
        out_specs=row,
        out_shape=jax.ShapeDtypeStruct((k, m), jnp.int32),
        compiler_params=_cparams(("parallel",)),
        name="moe_dest",
    )(idx, rank, gstart_col)


def _dispatch_kernel(d0_ref, d1_ref, pad0_ref, pad1_ref, h_ref, xs_ref, sem, *, tm):
    row0 = pl.program_id(0) * tm

    def row_copy(src_row, dst_row):
        return pltpu.make_async_copy(h_ref.at[pl.ds(src_row, 1)], xs_ref.at[pl.ds(dst_row, 1)], sem)

    def wait_rows(n):
        def body(r, carry):
            row_copy(0, 0).wait()
            return carry
        lax.fori_loop(0, n, body, 0)

    def scatter_token(r, carry):
        row_copy(r, d0_ref[row0 + r]).start()
        row_copy(r, d1_ref[row0 + r]).start()
        return carry

    lax.fori_loop(0, tm, scatter_token, 0)
    wait_rows(TOP_K * tm)

    @pl.when(pl.program_id(0) == 0)
    def _():
        for ex in range(N_EXPERTS):
            lo, hi = pad0_ref[ex], pad1_ref[ex]

            def fill(r, carry, lo=lo):
                row_copy(0, r).start()

                @pl.when(r - lo >= DMA_WINDOW)
                def _():
                    wait_rows(1)
                return carry

            lax.fori_loop(lo, hi, fill, 0)
            wait_rows(jnp.minimum(hi - lo, DMA_WINDOW))


def _dispatch(h, dest0, dest1, pad0, pad1, *, n_rows, tm=256):
    m, d = h.shape
    tm = _tile(m, tm, SUBLANES)
    return pl.pallas_call(
        functools.partial(_dispatch_kernel, tm=tm),
        grid_spec=pltpu.PrefetchScalarGridSpec(
            num_scalar_prefetch=4, grid=(m // tm,),
            in_specs=[pl.BlockSpec((tm, d), lambda i, *_: (i, 0))],
            out_specs=pl.BlockSpec(memory_space=pl.ANY),
            scratch_shapes=[pltpu.SemaphoreType.DMA(())]),
        out_shape=jax.ShapeDtypeStruct((n_rows, d), h.dtype),
        compiler_params=_cparams(("arbitrary",)),
        name="moe_dispatch",
    )(dest0, dest1, pad0, pad1, h)


def _expert_up_kernel(te_ref, used_ref, x_ref, wg_ref, wu_ref, o_ref):
    active = pl.program_id(1) < used_ref[0]

    @pl.when(active)
    def _():
        x = x_ref[...].astype(BF16)
        g = jnp.dot(x, wg_ref[...], preferred_element_type=F32)
        u = jnp.dot(x, wu_ref[...], preferred_element_type=F32)
        o_ref[...] = (_silu(g) * u).astype(o_ref.dtype)

    @pl.when(jnp.logical_not(active))
    def _():
        o_ref[...] = jnp.zeros_like(o_ref)


def _expert_up(xs, wg, wu, tile_expert, used, *, tn=512):
    r, d = xs.shape
    f = wg.shape[-1]
    tn = _tile(f, tn, LANES)
    wspec = pl.BlockSpec((None, d, tn), lambda j, i, te, us: (te[i], 0, j))
    return pl.pallas_call(
        _expert_up_kernel,
        grid_spec=pltpu.PrefetchScalarGridSpec(
            num_scalar_prefetch=2, grid=(f // tn, r // MOE_TM),
            in_specs=[pl.BlockSpec((MOE_TM, d), lambda j, i, te, us: (i, 0)), wspec, wspec],
            out_specs=pl.BlockSpec((MOE_TM, tn), lambda j, i, te, us: (i, j))),
        out_shape=jax.ShapeDtypeStruct((r, f), BF16),
        compiler_params=_cparams(("arbitrary", "arbitrary")),
        name="moe_up",
    )(tile_expert, used, xs, wg, wu)


def _expert_down_kernel(te_ref, used_ref, a_ref, wd_ref, o_ref):
    active = pl.program_id(1) < used_ref[0]

    @pl.when(active)
    def _():
        o_ref[...] = jnp.dot(a_ref[...], wd_ref[...], preferred_element_type=F32)

    @pl.when(jnp.logical_not(active))
    def _():
        o_ref[...] = jnp.zeros_like(o_ref)


def _expert_down(act, wd, tile_expert, used, *, tn=2048):
    r, f = act.shape
    d = wd.shape[-1]
    tn = _tile(d, tn, LANES)
    return pl.pallas_call(
        _expert_down_kernel,
        grid_spec=pltpu.PrefetchScalarGridSpec(
            num_scalar_prefetch=2, grid=(d // tn, r // MOE_TM),
            in_specs=[pl.BlockSpec((MOE_TM, f), lambda j, i, te, us: (i, 0)),
                      pl.BlockSpec((None, f, tn), lambda j, i, te, us: (te[i], 0, j))],
            out_specs=pl.BlockSpec((MOE_TM, tn), lambda j, i, te, us: (i, j))),
        out_shape=jax.ShapeDtypeStruct((r, d), F32),
        compiler_params=_cparams(("arbitrary", "arbitrary")),
        name="moe_down",
    )(tile_expert, used, act, wd)


def _rope_tables(n_batch, t_lat, t_ctx, width, lane0, scale):
    nf = MLA_ROPE // 4
    pos = jnp.arange(t_lat)
    inv = ROPE_THETA ** (-jnp.arange(nf, dtype=F32) / nf)
    ang_r = (pos // GRID_W).astype(F32)[:, None] * inv
    ang_c = (pos % GRID_W).astype(F32)[:, None] * inv
    z = jnp.zeros((t_lat, nf), F32)
    cos = jnp.concatenate([jnp.cos(ang_r)] * 2 + [jnp.cos(ang_c)] * 2, axis=1)
    s1 = jnp.concatenate([z, jnp.sin(ang_r), z, jnp.sin(ang_c)], axis=1)
    s2 = jnp.concatenate([-jnp.sin(ang_r), z, -jnp.sin(ang_c), z], axis=1)

    def place(rot, ctx_rot, fill):
        def rows(r, n):
            left = jnp.full((n, lane0), fill, F32)
            right = jnp.zeros((n, width - lane0 - MLA_ROPE), F32)
            return jnp.concatenate([left, r, right], axis=1)
        lat = rows(rot, t_lat)
        ctx = rows(jnp.broadcast_to(ctx_rot, (t_ctx, MLA_ROPE)), t_ctx)
        return jnp.concatenate([lat] * n_batch + [ctx] * n_batch, axis=0) * scale

    return (place(cos, jnp.ones((MLA_ROPE,), F32), 1.0), place(s1, jnp.zeros((MLA_ROPE,), F32), 0.0),
            place(s2, jnp.zeros((MLA_ROPE,), F32), 0.0))


def _prep_w_in(w):
    o = {}
    off = 0
    for name, width in (("cq", MLA_Q_RANK), ("ckv", MLA_KV_RANK), ("krope", MLA_ROPE), ("sc", 3 * SC_WIDTH),
                        ("gdn", 2 * GDN_QK + 2 * GDN_VW), ("ab", 2 * N_DIR * GDN_HEADS)):
        o[name] = (off, width)
        off += width
    sl = lambda n: w[:, o[n][0]:o[n][0] + o[n][1]]
    small = jnp.concatenate([sl("cq"), sl("ckv"), sl("krope"), sl("ab")], axis=1).astype(BF16)
    return small, sl("sc").astype(BF16), sl("gdn").astype(BF16), w[:, off:].astype(BF16)


def _prep_w_uq(w):
    r = w.shape[0]
    w3 = w.reshape(r, MLA_HEADS, MLA_QK)
    w3 = jnp.concatenate([w3, jnp.zeros((r, MLA_HEADS, MLA_PAD - MLA_QK), w.dtype)], axis=-1)
    return w3.reshape(r, MLA_HEADS * MLA_PAD).astype(BF16)


def _prep_w_ukv(w):
    r = w.shape[0]
    w3 = w.reshape(r, MLA_HEADS, MLA_NOPE + MLA_V)
    k_top = jnp.concatenate([w3[..., :MLA_NOPE], jnp.zeros((r, MLA_HEADS, MLA_PAD - MLA_NOPE), w.dtype)], -1)
    v_top = w3[..., MLA_NOPE:]
    eye = jnp.eye(MLA_ROPE, dtype=w.dtype)[:, None, :]
    k_rope = jnp.concatenate([jnp.zeros((MLA_ROPE, MLA_HEADS, MLA_NOPE), w.dtype),
                              jnp.broadcast_to(eye, (MLA_ROPE, MLA_HEADS, MLA_ROPE)),
                              jnp.zeros((MLA_ROPE, MLA_HEADS, MLA_PAD - MLA_QK), w.dtype)], -1)
    pad_rows = KV_LHS_W - r - MLA_ROPE
    keys = jnp.concatenate([k_top.reshape(r, -1), k_rope.reshape(MLA_ROPE, -1),
                            jnp.zeros((pad_rows, MLA_HEADS * MLA_PAD), w.dtype)], 0)
    vals = jnp.concatenate([v_top.reshape(r, -1), jnp.zeros((MLA_ROPE + pad_rows, MLA_HEADS * MLA_V), w.dtype)], 0)
    return jnp.concatenate([keys, vals], axis=1).astype(BF16)


def kernel(x, c, ctx, c_ctx, w_ada, b_ada, w_in, q_norm, kv_norm, w_uq, w_ukv, sc_conv, gdn_conv, gdn_a_log,
           gdn_dt_bias, gdn_norm, w_branch, w_o, ln1_g, ln1_b, ln2_g, ln2_b, w_router, router_bias, w_e_gate,
           w_e_up, w_e_down):
    n_batch, t_lat, d = x.shape
    t_ctx = ctx.shape[1]
    depth = w_ada.shape[0]
    n_lat = n_batch * t_lat
    n_all = n_lat + n_batch * t_ctx
    alpha = (2 * depth) ** 0.25
    geo = dict(t_lat=t_lat, n_batch=n_batch)

    xa = jnp.concatenate([x.reshape(n_lat, d), ctx.reshape(n_batch * t_ctx, d)], axis=0)
    cc = jnp.zeros((SUBLANES, d), F32).at[:n_batch].set(c).at[n_batch].set(c_ctx)
    mods = [
        _matmul(cc, w_ada, b_layer=l, out_dtype=F32, tm=SUBLANES, tn=512, lhs_silu=True,
                extra=((b_ada[l][None, :], (1, 512), lambda i, j: (0, j)),), epilogue=_ep_bias,
                name="adaln").reshape(SUBLANES, ADA_MULT, d)
        for l in range(depth)
    ]
    q_tabs = _rope_tables(n_batch, t_lat, t_ctx, MLA_PAD, MLA_NOPE, MLA_QK ** -0.5)
    k_tabs = _rope_tables(n_batch, t_lat, t_ctx, LANES, 0, 1.0)
    w_router_t = w_router.T.astype(BF16)
    bias_col = router_bias.reshape(-1, 1)
    lane_vec = lambda v: jnp.zeros((1, LANES), F32).at[0, 64:96].set(v.reshape(-1))

    h1 = _modulate(xa, mods[0], shift=0, scale=1, **geo)
    for l in range(depth):
        with_ctx = l < depth - 1
        m = n_all if with_ctx else n_lat
        w_small, w_sc, w_gdn, w_gates = _prep_w_in(w_in[l])

        wide = dict(tn=1024, cols_outer=True)
        p_small = _matmul(h1, w_small, out_dtype=F32, tn=SMALL_W, name="in_small")
        p_sc = _matmul(h1, w_sc, out_dtype=F32, m=m, name="in_sc", **wide)
        p_gdn = _matmul(h1, w_gdn, out_dtype=F32, name="in_gdn", **wide)
        gates = _matmul(h1, w_gates, out_dtype=BF16, m=m, epilogue=_ep_sigmoid, name="in_gates", **wide)

        cqn, kvl, gb = _small_prologue(p_small, q_norm[l][None, :], kv_norm[l][None, :], *k_tabs,
                                       lane_vec(-jnp.exp(gdn_a_log[l])), lane_vec(gdn_dt_bias[l]))
        q = _matmul(cqn, _prep_w_uq(w_uq[l]), out_dtype=BF16, m=m, tn=MLA_PAD, epilogue=_ep_rope,
                    extra=tuple((t, (_tile(m, 512, SUBLANES), MLA_PAD), lambda i, j: (i, 0)) for t in q_tabs),
                    name="q_up")
        kv = _matmul(kvl, _prep_w_ukv(w_ukv[l]), out_dtype=BF16, name="kv_up")
        attn = _attention(q, kv, n_batch=n_batch, t_lat=t_lat, t_ctx=t_ctx, latent_queries=True)
        if with_ctx:
            attn_ctx = _attention(q, kv, n_batch=n_batch, t_lat=t_lat, t_ctx=t_ctx, latent_queries=False)
            attn = jnp.concatenate([attn, attn_ctx], axis=0)

        conv = _short_conv(p_sc, sc_conv[l], m=m, n_lat=n_lat, t_lat=t_lat, t_ctx=t_ctx)

        feat = _gdn_features(p_gdn, gdn_conv[l], n_lat=n_lat, t_lat=t_lat, t_ctx=t_ctx)
        o0, o1 = _gdn_scan(feat, gb, _pack_decay_rows(gb), n_batch=n_batch, t_lat=t_lat, t_ctx=t_ctx)
        gdn = _gdn_output(o0, o1, p_gdn, gdn_norm[l][None, :], m=m)

        merged = _merge(attn, conv, gdn, w_branch[l].astype(BF16), gates, m=m)
        mix = _matmul(merged, w_o[l].astype(BF16), out_dtype=F32, name="w_o", **wide)
        x1, h2, logits = _ln_router(xa, mix, mods[l], ln1_g[l][None, :], ln1_b[l][None, :], w_router_t, m=m,
                                    alpha=alpha, **geo)

        idx, rank, wt, counts = _route(logits, bias_col)
        plan = _moe_plan(counts, m)
        dest = _dest(idx, rank, plan["gstart"])
        xs = _dispatch(h2, dest[0], dest[1], plan["pad0"], plan["pad1"], n_rows=plan["n_rows"])
        act = _expert_up(xs, w_e_gate[l].astype(BF16), w_e_up[l].astype(BF16), plan["tile_expert"], plan["used"])
        ys = _expert_down(act, w_e_down[l].astype(BF16), plan["tile_expert"], plan["used"])
        res = _ln_combine(x1, ys, dest[0], dest[1], wt, mods[l], ln2_g[l][None, :], ln2_b[l][None, :], m=m,
                          alpha=alpha, mod_next=mods[l + 1] if with_ctx else None, **geo)
        xa = res[0]
        if with_ctx:
            h1 = res[1]
    return xa[:n_lat].reshape(n_batch, t_lat, d)
```

```python
import functools
import math

import jax
import jax.numpy as jnp
from jax import lax
from jax.experimental import pallas as pl
from jax.experimental.pallas import tpu as pltpu

F32 = jnp.float32
BF16 = jnp.bfloat16

GRID_W = 64
MLA_HEADS = 16
MLA_Q_RANK = 1024
MLA_KV_RANK = 512
MLA_NOPE = 128
MLA_ROPE = 64
MLA_V = 128
MLA_QK = MLA_NOPE + MLA_ROPE
ATTN_HEADS = 2
MLA_PAD = 256
ROPE_THETA = 10000.0
SC_WIDTH = 2048
GDN_HEADS = 16
GDN_DK = 128
GDN_DV = 128
GDN_QK = GDN_HEADS * GDN_DK
GDN_VW = GDN_HEADS * GDN_DV
GDN_CHUNK = 64
N_DIR = 2
QUAD = 4
BRANCH_W = 2048
N_BRANCH = 3
ADA_MULT = 6
N_EXPERTS = 16
N_GROUPS = 4
EXPERTS_PER_GROUP = N_EXPERTS // N_GROUPS
TOP_K = 2
EPS = 1e-6
SMALL_W = MLA_Q_RANK + MLA_KV_RANK + 128
KV_LHS_W = MLA_KV_RANK + 128

LANES = 128
SUBLANES = 8
VMEM_LIMIT = 56 * 1024 * 1024
MOE_TM = 256
DMA_WINDOW = 256
HALO = SUBLANES


def _cparams(sem):
    return pltpu.CompilerParams(dimension_semantics=sem, vmem_limit_bytes=VMEM_LIMIT)


def _tile(dim, pref, align):
    if dim <= pref:
        return dim
    t = (pref // align) * align
    while t >= align:
        if dim % t == 0:
            return t
        t -= align
    return dim


def _silu(v):
    return v * jax.nn.sigmoid(v)


def _mm_kernel(*refs, nk, n_extra, epilogue, lhs_silu):
    a_ref, b_ref = refs[0], refs[1]
    extra = refs[2:2 + n_extra]
    o_ref = refs[2 + n_extra]
    a = a_ref[...]
    if lhs_silu:
        a = _silu(a.astype(F32))
    part = jnp.dot(a.astype(BF16), b_ref[...].astype(BF16), preferred_element_type=F32)
    if nk == 1:
        o_ref[...] = epilogue(part, *extra).astype(o_ref.dtype)
        return
    acc_ref = refs[3 + n_extra]
    k = pl.program_id(2)

    @pl.when(k == 0)
    def _():
        acc_ref[...] = part

    @pl.when(k > 0)
    def _():
        acc_ref[...] += part

    @pl.when(k == nk - 1)
    def _():
        o_ref[...] = epilogue(acc_ref[...], *extra).astype(o_ref.dtype)


def _matmul(a, b, *, out_dtype, m=None, tm=512, tn=512, tk=None, b_layer=None, extra=(),
            epilogue=None, lhs_silu=False, cols_outer=False, name="matmul"):
    m = a.shape[0] if m is None else m
    kdim = a.shape[1]
    n = b.shape[-1]
    tm = _tile(m, tm, SUBLANES)
    tn = _tile(n, tn, LANES)
    tk = kdim if tk is None else _tile(kdim, tk, LANES)
    nk = kdim // tk
    if epilogue is None:
        epilogue = lambda acc: acc
    ij = (lambda g0, g1: (g1, g0)) if cols_outer else (lambda g0, g1: (g0, g1))
    grid = (n // tn, m // tm, nk) if cols_outer else (m // tm, n // tn, nk)
    a_spec = pl.BlockSpec((tm, tk), lambda g0, g1, k: (ij(g0, g1)[0], k))
    if b.ndim == 3:
        b_spec = pl.BlockSpec((None, tk, tn), lambda g0, g1, k: (b_layer, k, ij(g0, g1)[1]))
    else:
        b_spec = pl.BlockSpec((tk, tn), lambda g0, g1, k: (k, ij(g0, g1)[1]))
    extra_specs = [pl.BlockSpec(bs, (lambda g0, g1, k, im=im: im(*ij(g0, g1)))) for _, bs, im in extra]
    scratch = [pltpu.VMEM((tm, tn), F32)] if nk > 1 else []
    return pl.pallas_call(
        functools.partial(_mm_kernel, nk=nk, n_extra=len(extra), epilogue=epilogue, lhs_silu=lhs_silu),
        grid=grid,
        in_specs=[a_spec, b_spec] + extra_specs,
        out_specs=pl.BlockSpec((tm, tn), lambda g0, g1, k: ij(g0, g1)),
        out_shape=jax.ShapeDtypeStruct((m, n), out_dtype),
        scratch_shapes=scratch,
        compiler_params=_cparams(("parallel", "parallel", "arbitrary")),
        name=name,
    )(a, b, *[e[0] for e in extra])


def _ep_bias(acc, bias_ref):
    return acc + bias_ref[...]


def _ep_sigmoid(acc):
    return jax.nn.sigmoid(acc)


def _ep_rope(acc, c_ref, s1_ref, s2_ref):
    n = acc.shape[1]
    reps = n // c_ref.shape[1]
    tile = lambda r: jnp.concatenate([r[...]] * reps, axis=1)
    return acc * tile(c_ref) + pltpu.roll(acc, 16, 1) * tile(s1_ref) + pltpu.roll(acc, n - 16, 1) * tile(s2_ref)


def _mod_row_map(tm, t_lat, n_batch):
    return lambda i, *_: (jnp.minimum((i * tm) // t_lat, n_batch), 0, 0)


def _modulate_kernel(x_ref, mod_ref, h_ref, *, shift, scale):
    x = x_ref[...]
    h_ref[...] = (x * (1.0 + mod_ref[scale:scale + 1, :]) + mod_ref[shift:shift + 1, :]).astype(h_ref.dtype)


def _modulate(xa, mod, *, shift, scale, t_lat, n_batch, tm=256):
    m, d = xa.shape
    tm = _tile(math.gcd(m, t_lat), tm, SUBLANES)
    return pl.pallas_call(
        functools.partial(_modulate_kernel, shift=shift, scale=scale),
        grid=(m // tm,),
        in_specs=[pl.BlockSpec((tm, d), lambda i: (i, 0)),
                  pl.BlockSpec((None, ADA_MULT, d), _mod_row_map(tm, t_lat, n_batch))],
        out_specs=pl.BlockSpec((tm, d), lambda i: (i, 0)),
        out_shape=jax.ShapeDtypeStruct((m, d), BF16),
        compiler_params=_cparams(("parallel",)),
        name="modulate",
    )(xa, mod)


def _post_norm(x, y, gate_row, lnw_ref, lnb_ref, alpha):
    v = alpha * x + gate_row * y
    vc = v - jnp.mean(v, axis=-1, keepdims=True)
    var = jnp.mean(vc * vc, axis=-1, keepdims=True)
    return vc * lax.rsqrt(var + EPS) * lnw_ref[...] + lnb_ref[...]


def _ln_router_kernel(x_ref, y_ref, mod_ref, lnw_ref, lnb_ref, wr_ref, xo_ref, h_ref, lg_ref, *, alpha):
    xn = _post_norm(x_ref[...], y_ref[...], mod_ref[2:3, :], lnw_ref, lnb_ref, alpha)
    xo_ref[...] = xn
    h = xn * (1.0 + mod_ref[4:5, :]) + mod_ref[3:4, :]
    h_ref[...] = h
    lg_ref[...] = lax.dot_general(wr_ref[...], h.astype(BF16), (((1,), (1,)), ((), ())),
                                  preferred_element_type=F32)


def _ln_router(xa, y, mod, lnw, lnb, w_router_t, *, m, alpha, t_lat, n_batch, tm=256):
    d = xa.shape[1]
    e = w_router_t.shape[0]
    tm = _tile(math.gcd(m, t_lat), tm, LANES)
    row = lambda i: (i, 0)
    const = lambda i: (0, 0)
    return pl.pallas_call(
        functools.partial(_ln_router_kernel, alpha=alpha),
        grid=(m // tm,),
        in_specs=[pl.BlockSpec((tm, d), row), pl.BlockSpec((tm, d), row),
                  pl.BlockSpec((None, ADA_MULT, d), _mod_row_map(tm, t_lat, n_batch)),
                  pl.BlockSpec((1, d), const), pl.BlockSpec((1, d), const), pl.BlockSpec((e, d), const)],
        out_specs=[pl.BlockSpec((tm, d), row), pl.BlockSpec((tm, d), row), pl.BlockSpec((e, tm), lambda i: (0, i))],
        out_shape=[jax.ShapeDtypeStruct((m, d), F32), jax.ShapeDtypeStruct((m, d), F32),
                   jax.ShapeDtypeStruct((e, m), F32)],
        compiler_params=_cparams(("parallel",)),
        name="ln_router",
    )(xa, y, mod, lnw, lnb, w_router_t)


def _ln_combine_kernel(d0_ref, d1_ref, x_ref, y_ref, wt_ref, mod_ref, lnw_ref, lnb_ref, *rest, alpha, tm,
                       with_h):
    if with_h:
        modn_ref, xo_ref, h_ref, buf, sem = rest
    else:
        xo_ref, buf, sem = rest
    row0 = pl.program_id(0) * tm

    def row_copy(src_row, slot, r):
        return pltpu.make_async_copy(y_ref.at[pl.ds(src_row, 1)], buf.at[slot, pl.ds(r, 1)], sem)

    def issue(r, carry):
        row_copy(d0_ref[row0 + r], 0, r).start()
        row_copy(d1_ref[row0 + r], 1, r).start()
        return carry

    lax.fori_loop(0, tm, issue, 0)

    def wait(r, carry):
        row_copy(0, 0, 0).wait()
        row_copy(0, 1, 0).wait()
        return carry

    lax.fori_loop(0, tm, wait, 0)
    ffn = wt_ref[:, 0:1] * buf[0] + wt_ref[:, 1:2] * buf[1]
    xn = _post_norm(x_ref[...], ffn, mod_ref[5:6, :], lnw_ref, lnb_ref, alpha)
    xo_ref[...] = xn
    if with_h:
        h_ref[...] = (xn * (1.0 + modn_ref[1:2, :]) + modn_ref[0:1, :]).astype(h_ref.dtype)


def _ln_combine(x1, y_sorted, dest0, dest1, wt, mod, lnw, lnb, *, m, alpha, t_lat, n_batch, mod_next=None,
                tm=256):
    d = x1.shape[1]
    tm = _tile(math.gcd(m, t_lat), tm, SUBLANES)
    with_h = mod_next is not None
    row = lambda i, *_: (i, 0)
    const = lambda i, *_: (0, 0)
    modmap = _mod_row_map(tm, t_lat, n_batch)
    in_specs = [pl.BlockSpec((tm, d), row), pl.BlockSpec(memory_space=pl.ANY), pl.BlockSpec((tm, LANES), row),
                pl.BlockSpec((None, ADA_MULT, d), modmap), pl.BlockSpec((1, d), const), pl.BlockSpec((1, d), const)]
    args = [x1, y_sorted, wt, mod, lnw, lnb]
    out_specs = [pl.BlockSpec((tm, d), row)]
    out_shape = [jax.ShapeDtypeStruct((m, d), F32)]
    if with_h:
        in_specs.append(pl.BlockSpec((None, ADA_MULT, d), modmap))
        args.append(mod_next)
        out_specs.append(pl.BlockSpec((tm, d), row))
        out_shape.append(jax.ShapeDtypeStruct((m, d), BF16))
    return pl.pallas_call(
        functools.partial(_ln_combine_kernel, alpha=alpha, tm=tm, with_h=with_h),
        grid_spec=pltpu.PrefetchScalarGridSpec(
            num_scalar_prefetch=2, grid=(m // tm,), in_specs=in_specs, out_specs=out_specs,
            scratch_shapes=[pltpu.VMEM((TOP_K, tm, d), F32), pltpu.SemaphoreType.DMA(())]),
        out_shape=out_shape,
        compiler_params=_cparams(("arbitrary",)),
        name="ln_combine",
    )(dest0, dest1, *args)


def _seq_edges(i, tm, n_lat, t_lat, t_ctx):
    row0 = i * tm
    in_lat = row0 < n_lat
    start = jnp.where(in_lat, row0 % t_lat == 0, (row0 - n_lat) % t_ctx == 0)
    end = jnp.where(in_lat, (row0 + tm) % t_lat == 0, (row0 + tm - n_lat) % t_ctx == 0)
    return start, end


def _conv3(u, prev_row, next_row, w_ref):
    tm = u.shape[0]
    rows = lax.broadcasted_iota(jnp.int32, u.shape, 0)
    um1 = jnp.where(rows == 0, prev_row, pltpu.roll(u, 1, 0))
    up1 = jnp.where(rows == tm - 1, next_row, pltpu.roll(u, tm - 1, 0))
    return w_ref[0:1, :] * um1 + w_ref[1:2, :] * u + w_ref[2:3, :] * up1


def _halo_specs(tm, tc, col_of_j, n_rows):
    last = n_rows // HALO - 1
    prev = pl.BlockSpec((HALO, tc), lambda i, j: (jnp.maximum(i * (tm // HALO) - 1, 0), col_of_j(j)))
    nxt = pl.BlockSpec((HALO, tc), lambda i, j: (jnp.minimum((i + 1) * (tm // HALO), last), col_of_j(j)))
    return prev, nxt


def _sconv_kernel(b_ref, c_ref, x_ref, cp_ref, xp_ref, cn_ref, xn_ref, w_ref, o_ref, *, tm, n_lat, t_lat,
                  t_ctx):
    start, end = _seq_edges(pl.program_id(0), tm, n_lat, t_lat, t_ctx)
    u = c_ref[...] * x_ref[...]
    prev = jnp.where(start, 0.0, cp_ref[HALO - 1:HALO, :] * xp_ref[HALO - 1:HALO, :])
    nxt = jnp.where(end, 0.0, cn_ref[0:1, :] * xn_ref[0:1, :])
    o_ref[...] = (b_ref[...] * _conv3(u, prev, nxt, w_ref)).astype(o_ref.dtype)


def _short_conv(p_sc, w, *, m, n_lat, t_lat, t_ctx, tm=256, tc=1024):
    n_rows = p_sc.shape[0]
    width = SC_WIDTH
    tc = _tile(width, tc, LANES)
    tm = _tile(math.gcd(t_lat, t_ctx), tm, HALO)
    nb = width // tc
    blk = lambda g: pl.BlockSpec((tm, tc), lambda i, j: (i, g * nb + j))
    cp, cn = _halo_specs(tm, tc, lambda j: nb + j, n_rows)
    xp, xn = _halo_specs(tm, tc, lambda j: 2 * nb + j, n_rows)
    return pl.pallas_call(
        functools.partial(_sconv_kernel, tm=tm, n_lat=n_lat, t_lat=t_lat, t_ctx=t_ctx),
        grid=(m // tm, nb),
        in_specs=[blk(0), blk(1), blk(2), cp, xp, cn, xn, pl.BlockSpec((3, tc), lambda i, j: (0, j))],
        out_specs=pl.BlockSpec((tm, tc), lambda i, j: (i, j)),
        out_shape=jax.ShapeDtypeStruct((m, width), BF16),
        compiler_params=_cparams(("parallel", "parallel")),
        name="short_conv",
    )(p_sc, p_sc, p_sc, p_sc, p_sc, p_sc, p_sc, w)


def _small_kernel(p_ref, qn_ref, kvn_ref, ck_ref, s1_ref, s2_ref, nexp_ref, dtb_ref, cq_ref, kv_ref, gb_ref):
    cq = p_ref[:, :MLA_Q_RANK]
    cq_ref[...] = (cq * lax.rsqrt(jnp.mean(cq * cq, axis=-1, keepdims=True) + EPS) * qn_ref[...]).astype(
        cq_ref.dtype)
    ckv = p_ref[:, MLA_Q_RANK:MLA_Q_RANK + MLA_KV_RANK]
    kv_ref[:, :MLA_KV_RANK] = (ckv * lax.rsqrt(jnp.mean(ckv * ckv, axis=-1, keepdims=True) + EPS)
                               * kvn_ref[...]).astype(kv_ref.dtype)
    x = p_ref[:, MLA_Q_RANK + MLA_KV_RANK:]
    rot = x * ck_ref[...] + pltpu.roll(x, 16, 1) * s1_ref[...] + pltpu.roll(x, LANES - 16, 1) * s2_ref[...]
    kv_ref[:, MLA_KV_RANK:] = rot.astype(kv_ref.dtype)
    z = x + dtb_ref[...]
    g = nexp_ref[...] * (jnp.maximum(z, 0.0) + jnp.log1p(jnp.exp(-jnp.abs(z))))
    tm = x.shape[0]
    rows = lax.broadcasted_iota(jnp.int32, x.shape, 0) % GDN_CHUNK
    lanes = lax.broadcasted_iota(jnp.int32, x.shape, 1)
    fwd = g
    rev = g
    s = 1
    while s < GDN_CHUNK:
        fwd = fwd + jnp.where(rows >= s, pltpu.roll(fwd, s, 0), 0.0)
        rev = rev + jnp.where(rows < GDN_CHUNK - s, pltpu.roll(rev, tm - s, 0), 0.0)
        s *= 2
    beta = jax.nn.sigmoid(x)
    gb_ref[...] = jnp.where(lanes < 64, 0.0, jnp.where(lanes < 80, fwd, jnp.where(lanes < 96, rev, beta)))


def _small_prologue(p_small, q_norm, kv_norm, ck, s1, s2, nexp, dtb, *, tm=256):
    m = p_small.shape[0]
    tm = _tile(m, tm, GDN_CHUNK)
    row = lambda i: (i, 0)
    vec = lambda w: pl.BlockSpec((1, w), lambda i: (0, 0))
    tab = pl.BlockSpec((tm, LANES), row)
    return pl.pallas_call(
        _small_kernel,
        grid=(m // tm,),
        in_specs=[pl.BlockSpec((tm, SMALL_W), row), vec(MLA_Q_RANK), vec(MLA_KV_RANK), tab, tab, tab,
                  vec(LANES), vec(LANES)],
        out_specs=[pl.BlockSpec((tm, MLA_Q_RANK), row), pl.BlockSpec((tm, KV_LHS_W), row),
                   pl.BlockSpec((tm, LANES), row)],
        out_shape=[jax.ShapeDtypeStruct((m, MLA_Q_RANK), BF16), jax.ShapeDtypeStruct((m, KV_LHS_W), BF16),
                   jax.ShapeDtypeStruct((m, LANES), F32)],
        compiler_params=_cparams(("parallel",)),
        name="small_prologue",
    )(p_small, q_norm, kv_norm, ck, s1, s2, nexp, dtb)


def _attn_kernel(*refs, with_lat):
    if with_lat:
        q_ref, kc_ref, vc_ref, kl_ref, vl_ref, o_ref = refs
    else:
        q_ref, kc_ref, vc_ref, o_ref = refs
    nt = (((1,), (1,)), ((), ()))
    heads = range(ATTN_HEADS)
    ks = lambda hh: slice(hh * MLA_PAD, (hh + 1) * MLA_PAD)
    vs = lambda hh: slice(hh * MLA_V, (hh + 1) * MLA_V)
    q = [q_ref[:, ks(hh)] for hh in heads]
    sc = [lax.dot_general(q[hh], kc_ref[:, ks(hh)], nt, preferred_element_type=F32) for hh in heads]
    mx = [jnp.max(sc[hh], axis=-1, keepdims=True) for hh in heads]
    if with_lat:
        sl = [lax.dot_general(q[hh], kl_ref[:, ks(hh)], nt, preferred_element_type=F32) for hh in heads]
        mx = [jnp.maximum(mx[hh], jnp.max(sl[hh], axis=-1, keepdims=True)) for hh in heads]
    pc = [jnp.exp(sc[hh] - mx[hh]) for hh in heads]
    den = [jnp.sum(pc[hh], axis=-1, keepdims=True) for hh in heads]
    acc = [jnp.dot(pc[hh].astype(BF16), vc_ref[:, vs(hh)], preferred_element_type=F32) for hh in heads]
    if with_lat:
        pl_ = [jnp.exp(sl[hh] - mx[hh]) for hh in heads]
        den = [den[hh] + jnp.sum(pl_[hh], axis=-1, keepdims=True) for hh in heads]
        acc = [acc[hh] + jnp.dot(pl_[hh].astype(BF16), vl_ref[:, vs(hh)], preferred_element_type=F32)
               for hh in heads]
    for hh in heads:
        o_ref[:, vs(hh)] = (acc[hh] / den[hh]).astype(o_ref.dtype)


def _attention(q, kv, *, n_batch, t_lat, t_ctx, latent_queries, tq=256):
    n_lat = n_batch * t_lat
    kw = ATTN_HEADS * MLA_PAD
    vw = ATTN_HEADS * MLA_V
    vcol0 = MLA_HEADS * MLA_PAD // vw
    if latent_queries:
        tq = _tile(t_lat, tq, SUBLANES)
        nq = t_lat // tq
        m_out = n_lat
        qrow = lambda b, hh, qi: (b * nq + qi, hh)
        orow = qrow
    else:
        tq = t_ctx
        nq = 1
        m_out = n_batch * t_ctx
        qrow = lambda b, hh, qi: (n_lat // t_ctx + b, hh)
        orow = lambda b, hh, qi: (b, hh)
    ctx_blk = n_lat // t_ctx
    in_specs = [pl.BlockSpec((tq, kw), qrow),
                pl.BlockSpec((t_ctx, kw), lambda b, hh, qi: (ctx_blk + b, hh)),
                pl.BlockSpec((t_ctx, vw), lambda b, hh, qi: (ctx_blk + b, vcol0 + hh))]
    args = [q, kv, kv]
    if latent_queries:
        in_specs += [pl.BlockSpec((t_lat, kw), lambda b, hh, qi: (b, hh)),
                     pl.BlockSpec((t_lat, vw), lambda b, hh, qi: (b, vcol0 + hh))]
        args += [kv, kv]
    return pl.pallas_call(
        functools.partial(_attn_kernel, with_lat=latent_queries),
        grid=(n_batch, MLA_HEADS // ATTN_HEADS, nq),
        in_specs=in_specs,
        out_specs=pl.BlockSpec((tq, vw), orow),
        out_shape=jax.ShapeDtypeStruct((m_out, MLA_HEADS * MLA_V), BF16),
        compiler_params=_cparams(("parallel", "parallel", "arbitrary")),
        name="attention_lat" if latent_queries else "attention_ctx",
    )(*args)


def _gdn_feat_kernel(x_ref, xp_ref, xn_ref, w_ref, o_ref, *, tm, tc, n_lat, t_lat, t_ctx):
    start, end = _seq_edges(pl.program_id(0), tm, n_lat, t_lat, t_ctx)
    j = pl.program_id(1)
    prev = jnp.where(start, 0.0, xp_ref[HALO - 1:HALO, :])
    nxt = jnp.where(end, 0.0, xn_ref[0:1, :])
    f = _silu(_conv3(x_ref[...], prev, nxt, w_ref))
    is_q = j < GDN_QK // tc
    is_v = j >= 2 * GDN_QK // tc
    post = jnp.where(is_q, GDN_DK ** -0.5, 1.0)
    for hh in range(tc // GDN_DK):
        fh = f[:, hh * GDN_DK:(hh + 1) * GDN_DK]
        inv = lax.rsqrt(jnp.sum(fh * fh, axis=-1, keepdims=True) + EPS) * post
        o_ref[:, hh * GDN_DK:(hh + 1) * GDN_DK] = fh * jnp.where(is_v, 1.0, inv)


def _gdn_features(p_gdn, w, *, n_lat, t_lat, t_ctx, tm=256, tc=1024):
    n_rows = p_gdn.shape[0]
    width = 2 * GDN_QK + GDN_VW
    tc = _tile(GDN_QK, tc, GDN_DK)
    tm = _tile(math.gcd(t_lat, t_ctx), tm, HALO)
    xp, xn = _halo_specs(tm, tc, lambda j: j, n_rows)
    return pl.pallas_call(
        functools.partial(_gdn_feat_kernel, tm=tm, tc=tc, n_lat=n_lat, t_lat=t_lat, t_ctx=t_ctx),
        grid=(n_rows // tm, width // tc),
        in_specs=[pl.BlockSpec((tm, tc), lambda i, j: (i, j)), xp, xn,
                  pl.BlockSpec((3, tc), lambda i, j: (0, j))],
        out_specs=pl.BlockSpec((tm, tc), lambda i, j: (i, j)),
        out_shape=jax.ShapeDtypeStruct((n_rows, width), F32),
        compiler_params=_cparams(("parallel", "parallel")),
        name="gdn_features",
    )(p_gdn, p_gdn, p_gdn, w)


def _block_diag(xp, blk, n):
    return jnp.concatenate([jnp.where(blk == p, xp, jnp.zeros_like(xp)) for p in range(n)], axis=0)


def _gdn_scan_kernel(fq0, fk0, fv0, gb0, gr0, fq1, fk1, fv1, gb1, gr1, o0_ref, o1_ref, state_ref):
    c = GDN_CHUNK
    pw = QUAD * c
    nt = (((1,), (1,)), ((), ()))
    tn = (((0,), (0,)), ((), ()))

    @pl.when(pl.program_id(1) == 0)
    def _():
        state_ref[...] = jnp.zeros_like(state_ref)

    ri = lax.broadcasted_iota(jnp.int32, (c, pw), 0)
    ci = lax.broadcasted_iota(jnp.int32, (c, pw), 1)
    cj = ci % c
    blk = ci // c
    blk_k = lax.broadcasted_iota(jnp.int32, (c, QUAD * GDN_DK), 1) // GDN_DK
    eye_p = jnp.where(ri == cj, 1.0, 0.0)
    lanes = lax.broadcasted_iota(jnp.int32, (c, LANES), 1)
    dirs = ((fq0, fk0, fv0, gb0, gr0), (fq1, fk1, fv1, gb1, gr1))
    nq = GDN_HEADS // QUAD

    quads = []
    for d, (fq, fk, fv, gb_ref, gr_ref) in enumerate(dirs):
        incl = (ri <= cj) if d == 1 else (ri >= cj)
        strict = (ri < cj) if d == 1 else (ri > cj)
        gbv = gb_ref[...]
        for g in range(nq):
            cols = slice(g * QUAD * GDN_DK, (g + 1) * QUAD * GDN_DK)
            k4, q4, v4 = fk[:, cols], fq[:, cols], fv[:, cols]
            grow = gr_ref[d * nq + g:d * nq + g + 1, :]
            heads, kb_l, gcol_p = [], [], None
            for p in range(QUAD):
                h = g * QUAD + p
                hs = slice(p * GDN_DK, (p + 1) * GDN_DK)
                gcol = jnp.sum(jnp.where(lanes == 64 + d * GDN_HEADS + h, gbv, 0.0), axis=1, keepdims=True)
                bcol = jnp.sum(jnp.where(lanes == 96 + d * GDN_HEADS + h, gbv, 0.0), axis=1, keepdims=True)
                gend = grow[:, p * c:p * c + 1] if d == 1 else grow[:, (p + 1) * c - 1:(p + 1) * c]
                egc = jnp.exp(gcol)
                kp = k4[:, hs]
                kb = kp * bcol
                kb_l.append(kb)
                heads.append(dict(
                    h=h,
                    rhs=jnp.concatenate([v4[:, hs] * bcol, kb * egc], axis=1).astype(BF16),
                    qe=q4[:, hs] * egc,
                    kt=(kp * jnp.exp(gend - gcol)).astype(BF16),
                    cd=jnp.exp(gend),
                ))
                gcol_p = gcol if gcol_p is None else jnp.where(blk == p, gcol, gcol_p)
            decay = jnp.exp(jnp.where(incl, gcol_p - grow, -jnp.inf))
            kbd = _block_diag(k4.astype(BF16), blk_k, QUAD)
            lhs = jnp.concatenate([jnp.concatenate(kb_l, axis=1), q4], axis=0).astype(BF16)
            quads.append(dict(d=d, heads=heads, decay=decay, strict=strict, kbd=kbd, lhs=lhs))

    for qd in quads:
        sc = lax.dot_general(qd["lhs"], qd["kbd"], nt, preferred_element_type=F32)
        qd["x"] = -jnp.where(qd["strict"], sc[:c] * qd["decay"], 0.0)
        qd["intra"] = (sc[c:] * qd["decay"]).astype(BF16)
        qd["tinv"] = eye_p + qd["x"]

    def split(v):
        hi = v.astype(BF16)
        return hi, (v - hi.astype(F32)).astype(BF16)

    def times_bd(v, bd_hi, bd_lo):
        hi, lo = split(v)
        r = jnp.dot(jnp.concatenate([hi, lo], axis=0), bd_hi, preferred_element_type=F32)
        return r[:c] + r[c:] + jnp.dot(hi, bd_lo, preferred_element_type=F32)

    def bd_split(v):
        hi, lo = split(v)
        return _block_diag(hi, blk, QUAD), _block_diag(lo, blk, QUAD)

    for qd in quads:
        qd["bd"] = bd_split(qd["x"])
    step = 2
    while step < c:
        for qd in quads:
            qd["x"] = times_bd(qd["x"], *qd["bd"])
        for qd in quads:
            qd["bd"] = bd_split(qd["x"])
        for qd in quads:
            qd["tinv"] = qd["tinv"] + times_bd(qd["tinv"], *qd["bd"])
        step *= 2

    for qd in quads:
        t16 = qd["tinv"].astype(BF16)
        for p, hd in enumerate(qd["heads"]):
            hd["uw"] = jnp.dot(t16[:, p * c:(p + 1) * c], hd["rhs"], preferred_element_type=F32)

    for qd in quads:
        for hd in qd["heads"]:
            hd["st"] = state_ref[qd["d"], hd["h"]]
            lhs = jnp.concatenate([hd["uw"][:, GDN_DV:], hd["qe"]], axis=0).astype(BF16)
            hd["ws"] = jnp.dot(lhs, hd["st"].astype(BF16), preferred_element_type=F32)
    for qd in quads:
        for hd in qd["heads"]:
            hd["vn"] = (hd["uw"][:, :GDN_DV] - hd["ws"][:c]).astype(BF16)
    outs = ([], [])
    for qd in quads:
        for p, hd in enumerate(qd["heads"]):
            outs[qd["d"]].append(hd["ws"][c:] + jnp.dot(qd["intra"][:, p * c:(p + 1) * c], hd["vn"],
                                                         preferred_element_type=F32))
            state_ref[qd["d"], hd["h"]] = hd["st"] * hd["cd"] + lax.dot_general(
                hd["kt"], hd["vn"], tn, preferred_element_type=F32)
    o0_ref[...] = jnp.concatenate(outs[0], axis=1)
    o1_ref[...] = jnp.concatenate(outs[1], axis=1)


def _gdn_scan(feat, gb, grows, *, n_batch, t_lat, t_ctx):
    n_rows = feat.shape[0]
    c = GDN_CHUNK
    n_lat = n_batch * t_lat
    nc_ctx, nc_lat = t_ctx // c, t_lat // c

    def chunk_row(d):
        def f(b, s):
            if d == 0:
                return jnp.where(s < nc_ctx, n_lat // c + b * nc_ctx + s, b * nc_lat + (s - nc_ctx))
            return jnp.where(s < nc_ctx, n_lat // c + b * nc_ctx + (nc_ctx - 1 - s),
                             b * nc_lat + (nc_lat - 1 - (s - nc_ctx)))
        return f

    in_specs, args = [], []
    for d in range(N_DIR):
        cr = chunk_row(d)
        for col0 in range(3):
            in_specs.append(pl.BlockSpec((c, GDN_QK), lambda b, s, cr=cr, col0=col0: (cr(b, s), col0)))
            args.append(feat)
        in_specs.append(pl.BlockSpec((c, LANES), lambda b, s, cr=cr: (cr(b, s), 0)))
        args.append(gb)
        in_specs.append(pl.BlockSpec((None,) + grows.shape[1:], lambda b, s, cr=cr: (cr(b, s), 0, 0)))
        args.append(grows)
    out_specs = [pl.BlockSpec((c, GDN_VW), lambda b, s, cr=chunk_row(d): (cr(b, s), 0)) for d in range(N_DIR)]
    return pl.pallas_call(
        _gdn_scan_kernel,
        grid=(n_batch, nc_ctx + nc_lat),
        in_specs=in_specs, out_specs=out_specs,
        out_shape=[jax.ShapeDtypeStruct((n_rows, GDN_VW), F32)] * N_DIR,
        scratch_shapes=[pltpu.VMEM((N_DIR, GDN_HEADS, GDN_DK, GDN_DV), F32)],
        compiler_params=_cparams(("parallel", "arbitrary")),
        name="gdn_scan",
    )(*args)


def _pack_decay_rows(gb):
    n_chunks = gb.shape[0] // GDN_CHUNK
    g = gb[:, 64:96].reshape(n_chunks, GDN_CHUNK, N_DIR, GDN_HEADS // QUAD, QUAD)
    return jnp.transpose(g, (0, 2, 3, 4, 1)).reshape(n_chunks, N_DIR * GDN_HEADS // QUAD, QUAD * GDN_CHUNK)


def _gdn_out_kernel(o0_ref, o1_ref, z_ref, w_ref, y_ref):
    o = o0_ref[...] + o1_ref[...]
    z = z_ref[...]
    for hh in range(o.shape[1] // GDN_DV):
        sl = slice(hh * GDN_DV, (hh + 1) * GDN_DV)
        oh = o[:, sl]
        yh = oh * lax.rsqrt(jnp.mean(oh * oh, axis=-1, keepdims=True) + EPS) * w_ref[...]
        y_ref[:, sl] = (yh * _silu(z[:, sl])).astype(y_ref.dtype)


def _gdn_output(o0, o1, p_gdn, w, *, m, tm=256, tc=1024):
    tc = _tile(GDN_VW, tc, GDN_DV)
    tm = _tile(m, tm, 16)
    z0 = (2 * GDN_QK + GDN_VW) // tc
    blk = pl.BlockSpec((tm, tc), lambda i, j: (i, j))
    return pl.pallas_call(
        _gdn_out_kernel,
        grid=(m // tm, GDN_VW // tc),
        in_specs=[blk, blk, pl.BlockSpec((tm, tc), lambda i, j: (i, z0 + j)),
                  pl.BlockSpec((1, GDN_DV), lambda i, j: (0, 0))],
        out_specs=blk,
        out_shape=jax.ShapeDtypeStruct((m, GDN_VW), BF16),
        compiler_params=_cparams(("parallel", "parallel")),
        name="gdn_output",
    )(o0, o1, p_gdn, w)


def _merge_kernel(a0, a1, a2, w_ref, g0, g1, g2, o_ref):
    acc = g0[...].astype(F32) * jnp.dot(a0[...], w_ref[0], preferred_element_type=F32)
    acc += g1[...].astype(F32) * jnp.dot(a1[...], w_ref[1], preferred_element_type=F32)
    acc += g2[...].astype(F32) * jnp.dot(a2[...], w_ref[2], preferred_element_type=F32)
    o_ref[...] = acc.astype(o_ref.dtype)


def _merge(attn, conv, gdn, w_branch, gates, *, m, tm=512, tn=512):
    d = w_branch.shape[-1]
    tm = _tile(m, tm, 16)
    tn = _tile(d, tn, LANES)
    nb = d // tn
    lhs = pl.BlockSpec((tm, BRANCH_W), lambda j, i: (i, 0))
    gate = lambda n: pl.BlockSpec((tm, tn), lambda j, i: (i, n * nb + j))
    return pl.pallas_call(
        _merge_kernel,
        grid=(nb, m // tm),
        in_specs=[lhs, lhs, lhs, pl.BlockSpec((N_BRANCH, BRANCH_W, tn), lambda j, i: (0, 0, j)),
                  gate(0), gate(1), gate(2)],
        out_specs=pl.BlockSpec((tm, tn), lambda j, i: (i, j)),
        out_shape=jax.ShapeDtypeStruct((m, d), BF16),
        compiler_params=_cparams(("parallel", "parallel")),
        name="merge",
    )(attn, conv, gdn, w_branch, gates, gates, gates)


def _route_kernel(lg_ref, bias_ref, idx_ref, rank_ref, wt_ref, cnt_ref, run_ref):
    @pl.when(pl.program_id(0) == 0)
    def _():
        run_ref[...] = jnp.zeros_like(run_ref)

    aff = jax.nn.sigmoid(lg_ref[...])
    sel = aff + bias_ref[...]
    e, tn = aff.shape
    eidx = lax.broadcasted_iota(jnp.int32, (e, tn), 0)
    best = None
    for g in range(N_GROUPS):
        r = [sel[g * EXPERTS_PER_GROUP + t:g * EXPERTS_PER_GROUP + t + 1, :] for t in range(EXPERTS_PER_GROUP)]
        top2 = None
        for a in range(EXPERTS_PER_GROUP):
            for b in range(a + 1, EXPERTS_PER_GROUP):
                s = r[a] + r[b]
                top2 = s if top2 is None else jnp.maximum(top2, s)
        if best is None:
            best, gidx = top2, jnp.zeros_like(top2, dtype=jnp.int32)
        else:
            better = top2 > best
            best = jnp.where(better, top2, best)
            gidx = jnp.where(better, g, gidx)
    masked = jnp.where(eidx // EXPERTS_PER_GROUP == gidx, sel, -jnp.inf)
    m1 = jnp.max(masked, axis=0, keepdims=True)
    i1 = jnp.min(jnp.where(masked == m1, eidx, e), axis=0, keepdims=True)
    masked2 = jnp.where(eidx == i1, -jnp.inf, masked)
    m2 = jnp.max(masked2, axis=0, keepdims=True)
    i2 = jnp.min(jnp.where(masked2 == m2, eidx, e), axis=0, keepdims=True)
    hot1 = eidx == i1
    hot2 = eidx == i2
    w1 = jnp.sum(jnp.where(hot1, aff, 0.0), axis=0, keepdims=True)
    w2 = jnp.sum(jnp.where(hot2, aff, 0.0), axis=0, keepdims=True)
    den = w1 + w2
    hot = jnp.where(hot1 | hot2, 1.0, 0.0)
    ra = lax.broadcasted_iota(jnp.int32, (tn, tn), 0)
    rb = lax.broadcasted_iota(jnp.int32, (tn, tn), 1)
    upper = jnp.where(ra <= rb, 1.0, 0.0).astype(BF16)
    incl = jnp.dot(hot.astype(BF16), upper, preferred_element_type=F32)
    pos = run_ref[...] + incl - hot
    rank1 = jnp.sum(jnp.where(hot1, pos, 0.0), axis=0, keepdims=True)
    rank2 = jnp.sum(jnp.where(hot2, pos, 0.0), axis=0, keepdims=True)
    run_ref[...] = run_ref[...] + incl[:, tn - 1:tn]
    idx_ref[...] = jnp.concatenate([i1, i2], axis=0)
    rank_ref[...] = jnp.concatenate([rank1, rank2], axis=0).astype(jnp.int32)
    wrow = jnp.concatenate([w1 / den, w2 / den, jnp.zeros((LANES - TOP_K, tn), F32)], axis=0)
    wt_ref[...] = wrow.T
    cnt_ref[...] = run_ref[...].astype(jnp.int32)


def _route(logits, bias_col, *, tn=512):
    e, m = logits.shape
    tn = _tile(m, tn, LANES)
    row2 = pl.BlockSpec((TOP_K, tn), lambda i: (0, i))
    return pl.pallas_call(
        _route_kernel,
        grid=(m // tn,),
        in_specs=[pl.BlockSpec((e, tn), lambda i: (0, i)), pl.BlockSpec((e, 1), lambda i: (0, 0))],
        out_specs=[row2, row2, pl.BlockSpec((tn, LANES), lambda i: (i, 0)), pl.BlockSpec((e, 1), lambda i: (0, 0))],
        out_shape=[jax.ShapeDtypeStruct((TOP_K, m), jnp.int32), jax.ShapeDtypeStruct((TOP_K, m), jnp.int32),
                   jax.ShapeDtypeStruct((m, LANES), F32), jax.ShapeDtypeStruct((e, 1), jnp.int32)],
        scratch_shapes=[pltpu.VMEM((e, 1), F32)],
        compiler_params=_cparams(("arbitrary",)),
        name="route",
    )(logits, bias_col)


def _moe_plan(counts, n_tokens):
    cnt = counts.reshape(-1)
    gsz = ((cnt + MOE_TM - 1) // MOE_TM) * MOE_TM
    gend = jnp.cumsum(gsz)
    gstart = gend - gsz
    nt = (TOP_K * n_tokens + MOE_TM - 1) // MOE_TM + N_EXPERTS
    used = (gend[-1] // MOE_TM).astype(jnp.int32)
    tile0 = jnp.arange(nt, dtype=jnp.int32) * MOE_TM
    te = jnp.minimum(jnp.sum(gend[None, :] <= tile0[:, None], axis=1), N_EXPERTS - 1).astype(jnp.int32)
    te = jnp.where(jnp.arange(nt) < used, te, te[jnp.maximum(used - 1, 0)])
    pad1 = gend.at[-1].set(nt * MOE_TM)
    return dict(gstart=gstart.astype(jnp.int32).reshape(-1, 1), pad0=(gstart + cnt).astype(jnp.int32),
                pad1=pad1.astype(jnp.int32), tile_expert=te, used=used.reshape(1), n_rows=nt * MOE_TM)


def _dest_kernel(idx_ref, rank_ref, start_ref, dest_ref):
    idx = idx_ref[...]
    out = rank_ref[...]
    for ex in range(N_EXPERTS):
        out = out + jnp.where(idx == ex, start_ref[ex:ex + 1, :], 0)
    dest_ref[...] = out


def _dest(idx, rank, gstart_col, *, tn=512):
    k, m = idx.shape
    tn = _tile(m, tn, LANES)
    row = pl.BlockSpec((k, tn), lambda i: (0, i))
    return pl.pallas_call(
        _dest_kernel,
        grid=(m // tn,),
        in_specs=[row, row, pl.BlockSpec((N_EXPERTS, 1), lambda i: (0, 0))],
        out_specs=row,
        out_shape=jax.ShapeDtypeStruct((k, m), jnp.int32),
        compiler_params=_cparams(("parallel",)),
        name="moe_dest",
    )(idx, rank, gstart_col)


def _dispatch_kernel(d0_ref, d1_ref, pad0_ref, pad1_ref, h_ref, xs_ref, sem, *, tm):
    row0 = pl.program_id(0) * tm

    def row_copy(src_row, dst_row):
        return pltpu.make_async_copy(h_ref.at[pl.ds(src_row, 1)], xs_ref.at[pl.ds(dst_row, 1)], sem)

    def wait_rows(n):
        def body(r, carry):
            row_copy(0, 0).wait()
            return carry
        lax.fori_loop(0, n, body, 0)

    def scatter_token(r, carry):
        row_copy(r, d0_ref[row0 + r]).start()
        row_copy(r, d1_ref[row0 + r]).start()
        return carry

    lax.fori_loop(0, tm, scatter_token, 0)
    wait_rows(TOP_K * tm)

    @pl.when(pl.program_id(0) == 0)
    def _():
        for ex in range(N_EXPERTS):
            lo, hi = pad0_ref[ex], pad1_ref[ex]

            def fill(r, carry, lo=lo):
                row_copy(0, r).start()

                @pl.when(r - lo >= DMA_WINDOW)
                def _():
                    wait_rows(1)
                return carry

            lax.fori_loop(lo, hi, fill, 0)
            wait_rows(jnp.minimum(hi - lo, DMA_WINDOW))


def _dispatch(h, dest0, dest1, pad0, pad1, *, n_rows, tm=256):
    m, d = h.shape
    tm = _tile(m, tm, SUBLANES)
    return pl.pallas_call(
        functools.partial(_dispatch_kernel, tm=tm),
        grid_spec=pltpu.PrefetchScalarGridSpec(
            num_scalar_prefetch=4, grid=(m // tm,),
            in_specs=[pl.BlockSpec((tm, d), lambda i, *_: (i, 0))],
            out_specs=pl.BlockSpec(memory_space=pl.ANY),
            scratch_shapes=[pltpu.SemaphoreType.DMA(())]),
        out_shape=jax.ShapeDtypeStruct((n_rows, d), h.dtype),
        compiler_params=_cparams(("arbitrary",)),
        name="moe_dispatch",
    )(dest0, dest1, pad0, pad1, h)


def _expert_up_kernel(te_ref, used_ref, x_ref, wg_ref, wu_ref, o_ref):
    active = pl.program_id(1) < used_ref[0]

    @pl.when(active)
    def _():
        x = x_ref[...].astype(BF16)
        g = jnp.dot(x, wg_ref[...], preferred_element_type=F32)
        u = jnp.dot(x, wu_ref[...], preferred_element_type=F32)
        o_ref[...] = (_silu(g) * u).astype(o_ref.dtype)

    @pl.when(jnp.logical_not(active))
    def _():
        o_ref[...] = jnp.zeros_like(o_ref)


def _expert_up(xs, wg, wu, layer, tile_expert, used, *, tn=512):
    r, d = xs.shape
    f = wg.shape[-1]
    tn = _tile(f, tn, LANES)
    wspec = pl.BlockSpec((None, None, d, tn), lambda j, i, te, us: (layer, te[i], 0, j))
    return pl.pallas_call(
        _expert_up_kernel,
        grid_spec=pltpu.PrefetchScalarGridSpec(
            num_scalar_prefetch=2, grid=(f // tn, r // MOE_TM),
            in_specs=[pl.BlockSpec((MOE_TM, d), lambda j, i, te, us: (i, 0)), wspec, wspec],
            out_specs=pl.BlockSpec((MOE_TM, tn), lambda j, i, te, us: (i, j))),
        out_shape=jax.ShapeDtypeStruct((r, f), BF16),
        compiler_params=_cparams(("arbitrary", "arbitrary")),
        name="moe_up",
    )(tile_expert, used, xs, wg, wu)


def _expert_down_kernel(te_ref, used_ref, a_ref, wd_ref, o_ref):
    active = pl.program_id(1) < used_ref[0]

    @pl.when(active)
    def _():
        o_ref[...] = jnp.dot(a_ref[...], wd_ref[...], preferred_element_type=F32)

    @pl.when(jnp.logical_not(active))
    def _():
        o_ref[...] = jnp.zeros_like(o_ref)


def _expert_down(act, wd, layer, tile_expert, used, *, tn=2048):
    r, f = act.shape
    d = wd.shape[-1]
    tn = _tile(d, tn, LANES)
    return pl.pallas_call(
        _expert_down_kernel,
        grid_spec=pltpu.PrefetchScalarGridSpec(
            num_scalar_prefetch=2, grid=(d // tn, r // MOE_TM),
            in_specs=[pl.BlockSpec((MOE_TM, f), lambda j, i, te, us: (i, 0)),
                      pl.BlockSpec((None, None, f, tn), lambda j, i, te, us: (layer, te[i], 0, j))],
            out_specs=pl.BlockSpec((MOE_TM, tn), lambda j, i, te, us: (i, j))),
        out_shape=jax.ShapeDtypeStruct((r, d), F32),
        compiler_params=_cparams(("arbitrary", "arbitrary")),
        name="moe_down",
    )(tile_expert, used, act, wd)


def _rope_tables(n_batch, t_lat, t_ctx, width, lane0, scale):
    nf = MLA_ROPE // 4
    pos = jnp.arange(t_lat)
    inv = ROPE_THETA ** (-jnp.arange(nf, dtype=F32) / nf)
    ang_r = (pos // GRID_W).astype(F32)[:, None] * inv
    ang_c = (pos % GRID_W).astype(F32)[:, None] * inv
    z = jnp.zeros((t_lat, nf), F32)
    cos = jnp.concatenate([jnp.cos(ang_r)] * 2 + [jnp.cos(ang_c)] * 2, axis=1)
    s1 = jnp.concatenate([z, jnp.sin(ang_r), z, jnp.sin(ang_c)], axis=1)
    s2 = jnp.concatenate([-jnp.sin(ang_r), z, -jnp.sin(ang_c), z], axis=1)

    def place(rot, ctx_rot, fill):
        def rows(r, n):
            left = jnp.full((n, lane0), fill, F32)
            right = jnp.zeros((n, width - lane0 - MLA_ROPE), F32)
            return jnp.concatenate([left, r, right], axis=1)
        lat = rows(rot, t_lat)
        ctx = rows(jnp.broadcast_to(ctx_rot, (t_ctx, MLA_ROPE)), t_ctx)
        return jnp.concatenate([lat] * n_batch + [ctx] * n_batch, axis=0) * scale

    return (place(cos, jnp.ones((MLA_ROPE,), F32), 1.0), place(s1, jnp.zeros((MLA_ROPE,), F32), 0.0),
            place(s2, jnp.zeros((MLA_ROPE,), F32), 0.0))


def _prep_w_in(w):
    o = {}
    off = 0
    for name, width in (("cq", MLA_Q_RANK), ("ckv", MLA_KV_RANK), ("krope", MLA_ROPE), ("sc", 3 * SC_WIDTH),
                        ("gdn", 2 * GDN_QK + 2 * GDN_VW), ("ab", 2 * N_DIR * GDN_HEADS)):
        o[name] = (off, width)
        off += width
    sl = lambda n: w[:, o[n][0]:o[n][0] + o[n][1]]
    small = jnp.concatenate([sl("cq"), sl("ckv"), sl("krope"), sl("ab")], axis=1).astype(BF16)
    return small, sl("sc").astype(BF16), sl("gdn").astype(BF16), w[:, off:].astype(BF16)


def _prep_w_uq(w):
    r = w.shape[0]
    w3 = w.reshape(r, MLA_HEADS, MLA_QK)
    w3 = jnp.concatenate([w3, jnp.zeros((r, MLA_HEADS, MLA_PAD - MLA_QK), w.dtype)], axis=-1)
    return w3.reshape(r, MLA_HEADS * MLA_PAD).astype(BF16)


def _prep_w_ukv(w):
    r = w.shape[0]
    w3 = w.reshape(r, MLA_HEADS, MLA_NOPE + MLA_V)
    k_top = jnp.concatenate([w3[..., :MLA_NOPE], jnp.zeros((r, MLA_HEADS, MLA_PAD - MLA_NOPE), w.dtype)], -1)
    v_top = w3[..., MLA_NOPE:]
    eye = jnp.eye(MLA_ROPE, dtype=w.dtype)[:, None, :]
    k_rope = jnp.concatenate([jnp.zeros((MLA_ROPE, MLA_HEADS, MLA_NOPE), w.dtype),
                              jnp.broadcast_to(eye, (MLA_ROPE, MLA_HEADS, MLA_ROPE)),
                              jnp.zeros((MLA_ROPE, MLA_HEADS, MLA_PAD - MLA_QK), w.dtype)], -1)
    pad_rows = KV_LHS_W - r - MLA_ROPE
    keys = jnp.concatenate([k_top.reshape(r, -1), k_rope.reshape(MLA_ROPE, -1),
                            jnp.zeros((pad_rows, MLA_HEADS * MLA_PAD), w.dtype)], 0)
    vals = jnp.concatenate([v_top.reshape(r, -1), jnp.zeros((MLA_ROPE + pad_rows, MLA_HEADS * MLA_V), w.dtype)], 0)
    return jnp.concatenate([keys, vals], axis=1).astype(BF16)


def kernel(x, c, ctx, c_ctx, w_ada, b_ada, w_in, q_norm, kv_norm, w_uq, w_ukv, sc_conv, gdn_conv, gdn_a_log,
           gdn_dt_bias, gdn_norm, w_branch, w_o, ln1_g, ln1_b, ln2_g, ln2_b, w_router, router_bias, w_e_gate,
           w_e_up, w_e_down):
    n_batch, t_lat, d = x.shape
    t_ctx = ctx.shape[1]
    depth = w_ada.shape[0]
    n_lat = n_batch * t_lat
    n_all = n_lat + n_batch * t_ctx
    alpha = (2 * depth) ** 0.25
    geo = dict(t_lat=t_lat, n_batch=n_batch)

    xa = jnp.concatenate([x.reshape(n_lat, d), ctx.reshape(n_batch * t_ctx, d)], axis=0)
    cc = jnp.zeros((SUBLANES, d), F32).at[:n_batch].set(c).at[n_batch].set(c_ctx)
    mods = [
        _matmul(cc, w_ada, b_layer=l, out_dtype=F32, tm=SUBLANES, tn=512, lhs_silu=True,
                extra=((b_ada[l][None, :], (1, 512), lambda i, j: (0, j)),), epilogue=_ep_bias,
                name="adaln").reshape(SUBLANES, ADA_MULT, d)
        for l in range(depth)
    ]
    q_tabs = _rope_tables(n_batch, t_lat, t_ctx, MLA_PAD, MLA_NOPE, MLA_QK ** -0.5)
    k_tabs = _rope_tables(n_batch, t_lat, t_ctx, LANES, 0, 1.0)
    w_router_t = w_router.T.astype(BF16)
    bias_col = router_bias.reshape(-1, 1)
    wg16, wu16, wd16 = w_e_gate.astype(BF16), w_e_up.astype(BF16), w_e_down.astype(BF16)
    lane_vec = lambda v: jnp.zeros((1, LANES), F32).at[0, 64:96].set(v.reshape(-1))

    h1 = _modulate(xa, mods[0], shift=0, scale=1, **geo)
    for l in range(depth):
        with_ctx = l < depth - 1
        m = n_all if with_ctx else n_lat
        w_small, w_sc, w_gdn, w_gates = _prep_w_in(w_in[l])

        wide = dict(tn=1024, cols_outer=True)
        p_small = _matmul(h1, w_small, out_dtype=F32, tn=SMALL_W, name="in_small")
        p_sc = _matmul(h1, w_sc, out_dtype=F32, m=m, name="in_sc", **wide)
        p_gdn = _matmul(h1, w_gdn, out_dtype=F32, name="in_gdn", **wide)
        gates = _matmul(h1, w_gates, out_dtype=BF16, m=m, epilogue=_ep_sigmoid, name="in_gates", **wide)

        cqn, kvl, gb = _small_prologue(p_small, q_norm[l][None, :], kv_norm[l][None, :], *k_tabs,
                                       lane_vec(-jnp.exp(gdn_a_log[l])), lane_vec(gdn_dt_bias[l]))
        q = _matmul(cqn, _prep_w_uq(w_uq[l]), out_dtype=BF16, m=m, tn=4 * MLA_PAD, epilogue=_ep_rope,
                    extra=tuple((t, (_tile(m, 512, SUBLANES), MLA_PAD), lambda i, j: (i, 0)) for t in q_tabs),
                    name="q_up")
        kv = _matmul(kvl, _prep_w_ukv(w_ukv[l]), out_dtype=BF16, tn=2048, name="kv_up")
        attn = _attention(q, kv, n_batch=n_batch, t_lat=t_lat, t_ctx=t_ctx, latent_queries=True)
        if with_ctx:
            attn_ctx = _attention(q, kv, n_batch=n_batch, t_lat=t_lat, t_ctx=t_ctx, latent_queries=False)
            attn = jnp.concatenate([attn, attn_ctx], axis=0)

        conv = _short_conv(p_sc, sc_conv[l], m=m, n_lat=n_lat, t_lat=t_lat, t_ctx=t_ctx)

        feat = _gdn_features(p_gdn, gdn_conv[l], n_lat=n_lat, t_lat=t_lat, t_ctx=t_ctx)
        o0, o1 = _gdn_scan(feat, gb, _pack_decay_rows(gb), n_batch=n_batch, t_lat=t_lat, t_ctx=t_ctx)
        gdn = _gdn_output(o0, o1, p_gdn, gdn_norm[l][None, :], m=m)

        merged = _merge(attn, conv, gdn, w_branch[l].astype(BF16), gates, m=m)
        mix = _matmul(merged, w_o[l].astype(BF16), out_dtype=F32, name="w_o", **wide)
        x1, h2, logits = _ln_router(xa, mix, mods[l], ln1_g[l][None, :], ln1_b[l][None, :], w_router_t, m=m,
                                    alpha=alpha, **geo)

        idx, rank, wt, counts = _route(logits, bias_col)
        plan = _moe_plan(counts, m)
        dest = _dest(idx, rank, plan["gstart"])
        xs = _dispatch(h2, dest[0], dest[1], plan["pad0"], plan["pad1"], n_rows=plan["n_rows"])
        act = _expert_up(xs, wg16, wu16, l, plan["tile_expert"], plan["used"])
        ys = _expert_down(act, wd16, l, plan["tile_expert"], plan["used"])
        res = _ln_combine(x1, ys, dest[0], dest[1], wt, mods[l], ln2_g[l][None, :], ln2_b[l][None, :], m=m,
                          alpha=alpha, mod_next=mods[l + 1] if with_ctx else None, **geo)
        xa = res[0]
        if with_ctx:
            h1 = res[1]
    return xa[:n_lat].reshape(n_batch, t_lat, d)
```

```python
import functools
import math

import jax
import jax.numpy as jnp
from jax import lax
from jax.experimental import pallas as pl
from jax.experimental.pallas import tpu as pltpu

F32 = jnp.float32
BF16 = jnp.bfloat16

GRID_W = 64
MLA_HEADS = 16
MLA_Q_RANK = 1024
MLA_KV_RANK = 512
MLA_NOPE = 128
MLA_ROPE = 64
MLA_V = 128
MLA_QK = MLA_NOPE + MLA_ROPE
ATTN_HEADS = 2
MLA_PAD = 256
ROPE_THETA = 10000.0
SC_WIDTH = 2048
GDN_HEADS = 16
GDN_DK = 128
GDN_DV = 128
GDN_QK = GDN_HEADS * GDN_DK
GDN_VW = GDN_HEADS * GDN_DV
GDN_CHUNK = 64
N_DIR = 2
QUAD = 4
BRANCH_W = 2048
N_BRANCH = 3
ADA_MULT = 6
N_EXPERTS = 16
N_GROUPS = 4
EXPERTS_PER_GROUP = N_EXPERTS // N_GROUPS
TOP_K = 2
EPS = 1e-6
SMALL_W = MLA_Q_RANK + MLA_KV_RANK + 128
KV_LHS_W = MLA_KV_RANK + 128

LANES = 128
SUBLANES = 8
VMEM_LIMIT = 56 * 1024 * 1024
MOE_TM = 256
DMA_WINDOW = 256
HALO = SUBLANES


def _cparams(sem):
    return pltpu.CompilerParams(dimension_semantics=sem, vmem_limit_bytes=VMEM_LIMIT)


def _tile(dim, pref, align):
    if dim <= pref:
        return dim
    t = (pref // align) * align
    while t >= align:
        if dim % t == 0:
            return t
        t -= align
    return dim


def _silu(v):
    return v * jax.nn.sigmoid(v)


def _mm_kernel(*refs, nk, n_extra, epilogue, lhs_silu):
    a_ref, b_ref = refs[0], refs[1]
    extra = refs[2:2 + n_extra]
    o_ref = refs[2 + n_extra]
    a = a_ref[...]
    if lhs_silu:
        a = _silu(a.astype(F32))
    part = jnp.dot(a.astype(BF16), b_ref[...].astype(BF16), preferred_element_type=F32)
    if nk == 1:
        o_ref[...] = epilogue(part, *extra).astype(o_ref.dtype)
        return
    acc_ref = refs[3 + n_extra]
    k = pl.program_id(2)

    @pl.when(k == 0)
    def _():
        acc_ref[...] = part

    @pl.when(k > 0)
    def _():
        acc_ref[...] += part

    @pl.when(k == nk - 1)
    def _():
        o_ref[...] = epilogue(acc_ref[...], *extra).astype(o_ref.dtype)


def _matmul(a, b, *, out_dtype, m=None, tm=512, tn=512, tk=None, b_layer=None, extra=(),
            epilogue=None, lhs_silu=False, cols_outer=False, cols=None, name="matmul"):
    m = a.shape[0] if m is None else m
    kdim = a.shape[1]
    col0, n = (0, b.shape[-1]) if cols is None else cols
    tm = _tile(m, tm, SUBLANES)
    tn = _tile(math.gcd(n, col0), tn, LANES)
    jb = col0 // tn
    tk = kdim if tk is None else _tile(kdim, tk, LANES)
    nk = kdim // tk
    if epilogue is None:
        epilogue = lambda acc: acc
    ij = (lambda g0, g1: (g1, g0)) if cols_outer else (lambda g0, g1: (g0, g1))
    grid = (n // tn, m // tm, nk) if cols_outer else (m // tm, n // tn, nk)
    a_spec = pl.BlockSpec((tm, tk), lambda g0, g1, k: (ij(g0, g1)[0], k))
    if b.ndim == 3:
        b_spec = pl.BlockSpec((None, tk, tn), lambda g0, g1, k: (b_layer, k, jb + ij(g0, g1)[1]))
    else:
        b_spec = pl.BlockSpec((tk, tn), lambda g0, g1, k: (k, jb + ij(g0, g1)[1]))
    extra_specs = [pl.BlockSpec(bs, (lambda g0, g1, k, im=im: im(*ij(g0, g1)))) for _, bs, im in extra]
    scratch = [pltpu.VMEM((tm, tn), F32)] if nk > 1 else []
    return pl.pallas_call(
        functools.partial(_mm_kernel, nk=nk, n_extra=len(extra), epilogue=epilogue, lhs_silu=lhs_silu),
        grid=grid,
        in_specs=[a_spec, b_spec] + extra_specs,
        out_specs=pl.BlockSpec((tm, tn), lambda g0, g1, k: ij(g0, g1)),
        out_shape=jax.ShapeDtypeStruct((m, n), out_dtype),
        scratch_shapes=scratch,
        compiler_params=_cparams(("parallel", "parallel", "arbitrary")),
        name=name,
    )(a, b, *[e[0] for e in extra])


def _ep_bias(acc, bias_ref):
    return acc + bias_ref[...]


def _ep_sigmoid(acc):
    return jax.nn.sigmoid(acc)


def _ep_rope(acc, c_ref, s1_ref, s2_ref):
    n = acc.shape[1]
    reps = n // c_ref.shape[1]
    tile = lambda r: jnp.concatenate([r[...]] * reps, axis=1)
    return acc * tile(c_ref) + pltpu.roll(acc, 16, 1) * tile(s1_ref) + pltpu.roll(acc, n - 16, 1) * tile(s2_ref)


def _mod_row_map(tm, t_lat, n_batch):
    return lambda i, *_: (jnp.minimum((i * tm) // t_lat, n_batch), 0, 0)


def _modulate_kernel(x_ref, mod_ref, h_ref, *, shift, scale):
    x = x_ref[...]
    h_ref[...] = (x * (1.0 + mod_ref[scale:scale + 1, :]) + mod_ref[shift:shift + 1, :]).astype(h_ref.dtype)


def _modulate(xa, mod, *, shift, scale, t_lat, n_batch, tm=256):
    m, d = xa.shape
    tm = _tile(math.gcd(m, t_lat), tm, SUBLANES)
    return pl.pallas_call(
        functools.partial(_modulate_kernel, shift=shift, scale=scale),
        grid=(m // tm,),
        in_specs=[pl.BlockSpec((tm, d), lambda i: (i, 0)),
                  pl.BlockSpec((None, ADA_MULT, d), _mod_row_map(tm, t_lat, n_batch))],
        out_specs=pl.BlockSpec((tm, d), lambda i: (i, 0)),
        out_shape=jax.ShapeDtypeStruct((m, d), BF16),
        compiler_params=_cparams(("parallel",)),
        name="modulate",
    )(xa, mod)


def _post_norm(x, y, gate_row, lnw_ref, lnb_ref, alpha):
    v = alpha * x + gate_row * y
    vc = v - jnp.mean(v, axis=-1, keepdims=True)
    var = jnp.mean(vc * vc, axis=-1, keepdims=True)
    return vc * lax.rsqrt(var + EPS) * lnw_ref[...] + lnb_ref[...]


def _ln_router_kernel(x_ref, y_ref, mod_ref, lnw_ref, lnb_ref, wr_ref, xo_ref, h_ref, lg_ref, *, alpha):
    xn = _post_norm(x_ref[...], y_ref[...], mod_ref[2:3, :], lnw_ref, lnb_ref, alpha)
    xo_ref[...] = xn
    h = xn * (1.0 + mod_ref[4:5, :]) + mod_ref[3:4, :]
    h_ref[...] = h
    lg_ref[...] = lax.dot_general(wr_ref[...], h.astype(BF16), (((1,), (1,)), ((), ())),
                                  preferred_element_type=F32)


def _ln_router(xa, y, mod, lnw, lnb, w_router_t, *, m, alpha, t_lat, n_batch, tm=256):
    d = xa.shape[1]
    e = w_router_t.shape[0]
    tm = _tile(math.gcd(m, t_lat), tm, LANES)
    row = lambda i: (i, 0)
    const = lambda i: (0, 0)
    return pl.pallas_call(
        functools.partial(_ln_router_kernel, alpha=alpha),
        grid=(m // tm,),
        in_specs=[pl.BlockSpec((tm, d), row), pl.BlockSpec((tm, d), row),
                  pl.BlockSpec((None, ADA_MULT, d), _mod_row_map(tm, t_lat, n_batch)),
                  pl.BlockSpec((1, d), const), pl.BlockSpec((1, d), const), pl.BlockSpec((e, d), const)],
        out_specs=[pl.BlockSpec((tm, d), row), pl.BlockSpec((tm, d), row), pl.BlockSpec((e, tm), lambda i: (0, i))],
        out_shape=[jax.ShapeDtypeStruct((m, d), F32), jax.ShapeDtypeStruct((m, d), F32),
                   jax.ShapeDtypeStruct((e, m), F32)],
        compiler_params=_cparams(("parallel",)),
        name="ln_router",
    )(xa, y, mod, lnw, lnb, w_router_t)


def _ln_combine_kernel(d0_ref, d1_ref, x_ref, y_ref, wt_ref, mod_ref, lnw_ref, lnb_ref, *rest, alpha, tm,
                       with_h):
    if with_h:
        modn_ref, xo_ref, h_ref, buf, sem = rest
    else:
        xo_ref, buf, sem = rest
    i = pl.program_id(0)
    n = pl.num_programs(0)

    def row_copy(src_row, par, slot, r):
        return pltpu.make_async_copy(y_ref.at[pl.ds(src_row, 1)], buf.at[par, slot, pl.ds(r, 1)], sem.at[par])

    def gather(tile):
        def issue(r, carry):
            row_copy(d0_ref[tile * tm + r], tile % 2, 0, r).start()
            row_copy(d1_ref[tile * tm + r], tile % 2, 1, r).start()
            return carry
        lax.fori_loop(0, tm, issue, 0)

    @pl.when(i == 0)
    def _():
        gather(i)

    @pl.when(i + 1 < n)
    def _():
        gather(i + 1)

    par = i % 2

    def wait(r, carry):
        row_copy(0, par, 0, 0).wait()
        row_copy(0, par, 1, 0).wait()
        return carry

    lax.fori_loop(0, tm, wait, 0)
    ffn = wt_ref[:, 0:1] * buf[par, 0] + wt_ref[:, 1:2] * buf[par, 1]
    xn = _post_norm(x_ref[...], ffn, mod_ref[5:6, :], lnw_ref, lnb_ref, alpha)
    xo_ref[...] = xn
    if with_h:
        h_ref[...] = (xn * (1.0 + modn_ref[1:2, :]) + modn_ref[0:1, :]).astype(h_ref.dtype)


def _ln_combine(x1, y_sorted, dest0, dest1, wt, mod, lnw, lnb, *, m, alpha, t_lat, n_batch, mod_next=None,
                tm=256):
    d = x1.shape[1]
    tm = _tile(math.gcd(m, t_lat), tm, SUBLANES)
    with_h = mod_next is not None
    row = lambda i, *_: (i, 0)
    const = lambda i, *_: (0, 0)
    modmap = _mod_row_map(tm, t_lat, n_batch)
    in_specs = [pl.BlockSpec((tm, d), row), pl.BlockSpec(memory_space=pl.ANY), pl.BlockSpec((tm, LANES), row),
                pl.BlockSpec((None, ADA_MULT, d), modmap), pl.BlockSpec((1, d), const), pl.BlockSpec((1, d), const)]
    args = [x1, y_sorted, wt, mod, lnw, lnb]
    out_specs = [pl.BlockSpec((tm, d), row)]
    out_shape = [jax.ShapeDtypeStruct((m, d), F32)]
    if with_h:
        in_specs.append(pl.BlockSpec((None, ADA_MULT, d), modmap))
        args.append(mod_next)
        out_specs.append(pl.BlockSpec((tm, d), row))
        out_shape.append(jax.ShapeDtypeStruct((m, d), BF16))
    return pl.pallas_call(
        functools.partial(_ln_combine_kernel, alpha=alpha, tm=tm, with_h=with_h),
        grid_spec=pltpu.PrefetchScalarGridSpec(
            num_scalar_prefetch=2, grid=(m // tm,), in_specs=in_specs, out_specs=out_specs,
            scratch_shapes=[pltpu.VMEM((2, TOP_K, tm, d), F32), pltpu.SemaphoreType.DMA((2,))]),
        out_shape=out_shape,
        compiler_params=_cparams(("arbitrary",)),
        name="ln_combine",
    )(dest0, dest1, *args)


def _seq_edges(i, tm, n_lat, t_lat, t_ctx):
    row0 = i * tm
    in_lat = row0 < n_lat
    start = jnp.where(in_lat, row0 % t_lat == 0, (row0 - n_lat) % t_ctx == 0)
    end = jnp.where(in_lat, (row0 + tm) % t_lat == 0, (row0 + tm - n_lat) % t_ctx == 0)
    return start, end


def _conv3(u, prev_row, next_row, w_ref):
    tm = u.shape[0]
    rows = lax.broadcasted_iota(jnp.int32, u.shape, 0)
    um1 = jnp.where(rows == 0, prev_row, pltpu.roll(u, 1, 0))
    up1 = jnp.where(rows == tm - 1, next_row, pltpu.roll(u, tm - 1, 0))
    return w_ref[0:1, :] * um1 + w_ref[1:2, :] * u + w_ref[2:3, :] * up1


def _halo_specs(tm, tc, col_of_j, n_rows):
    last = n_rows // HALO - 1
    prev = pl.BlockSpec((HALO, tc), lambda i, j: (jnp.maximum(i * (tm // HALO) - 1, 0), col_of_j(j)))
    nxt = pl.BlockSpec((HALO, tc), lambda i, j: (jnp.minimum((i + 1) * (tm // HALO), last), col_of_j(j)))
    return prev, nxt


def _sconv_kernel(b_ref, c_ref, x_ref, cp_ref, xp_ref, cn_ref, xn_ref, w_ref, o_ref, *, tm, n_lat, t_lat,
                  t_ctx):
    start, end = _seq_edges(pl.program_id(0), tm, n_lat, t_lat, t_ctx)
    u = c_ref[...] * x_ref[...]
    prev = jnp.where(start, 0.0, cp_ref[HALO - 1:HALO, :] * xp_ref[HALO - 1:HALO, :])
    nxt = jnp.where(end, 0.0, cn_ref[0:1, :] * xn_ref[0:1, :])
    o_ref[...] = (b_ref[...] * _conv3(u, prev, nxt, w_ref)).astype(o_ref.dtype)


def _short_conv(p_sc, w, *, m, n_lat, t_lat, t_ctx, tm=256, tc=1024):
    n_rows = p_sc.shape[0]
    width = SC_WIDTH
    tc = _tile(width, tc, LANES)
    tm = _tile(math.gcd(t_lat, t_ctx), tm, HALO)
    nb = width // tc
    blk = lambda g: pl.BlockSpec((tm, tc), lambda i, j: (i, g * nb + j))
    cp, cn = _halo_specs(tm, tc, lambda j: nb + j, n_rows)
    xp, xn = _halo_specs(tm, tc, lambda j: 2 * nb + j, n_rows)
    return pl.pallas_call(
        functools.partial(_sconv_kernel, tm=tm, n_lat=n_lat, t_lat=t_lat, t_ctx=t_ctx),
        grid=(m // tm, nb),
        in_specs=[blk(0), blk(1), blk(2), cp, xp, cn, xn, pl.BlockSpec((3, tc), lambda i, j: (0, j))],
        out_specs=pl.BlockSpec((tm, tc), lambda i, j: (i, j)),
        out_shape=jax.ShapeDtypeStruct((m, width), BF16),
        compiler_params=_cparams(("parallel", "parallel")),
        name="short_conv",
    )(p_sc, p_sc, p_sc, p_sc, p_sc, p_sc, p_sc, w)


def _small_kernel(p_ref, qn_ref, kvn_ref, ck_ref, s1_ref, s2_ref, nexp_ref, dtb_ref, cq_ref, kv_ref, gb_ref):
    cq = p_ref[:, :MLA_Q_RANK]
    cq_ref[...] = (cq * lax.rsqrt(jnp.mean(cq * cq, axis=-1, keepdims=True) + EPS) * qn_ref[...]).astype(
        cq_ref.dtype)
    ckv = p_ref[:, MLA_Q_RANK:MLA_Q_RANK + MLA_KV_RANK]
    kv_ref[:, :MLA_KV_RANK] = (ckv * lax.rsqrt(jnp.mean(ckv * ckv, axis=-1, keepdims=True) + EPS)
                               * kvn_ref[...]).astype(kv_ref.dtype)
    x = p_ref[:, MLA_Q_RANK + MLA_KV_RANK:]
    rot = x * ck_ref[...] + pltpu.roll(x, 16, 1) * s1_ref[...] + pltpu.roll(x, LANES - 16, 1) * s2_ref[...]
    kv_ref[:, MLA_KV_RANK:] = rot.astype(kv_ref.dtype)
    z = x + dtb_ref[...]
    g = nexp_ref[...] * (jnp.maximum(z, 0.0) + jnp.log1p(jnp.exp(-jnp.abs(z))))
    tm = x.shape[0]
    rows = lax.broadcasted_iota(jnp.int32, x.shape, 0) % GDN_CHUNK
    lanes = lax.broadcasted_iota(jnp.int32, x.shape, 1)
    fwd = g
    rev = g
    s = 1
    while s < GDN_CHUNK:
        fwd = fwd + jnp.where(rows >= s, pltpu.roll(fwd, s, 0), 0.0)
        rev = rev + jnp.where(rows < GDN_CHUNK - s, pltpu.roll(rev, tm - s, 0), 0.0)
        s *= 2
    beta = jax.nn.sigmoid(x)
    gb_ref[...] = jnp.where(lanes < 64, 0.0, jnp.where(lanes < 80, fwd, jnp.where(lanes < 96, rev, beta)))


def _small_prologue(p_small, q_norm, kv_norm, ck, s1, s2, nexp, dtb, *, tm=256):
    m = p_small.shape[0]
    tm = _tile(m, tm, GDN_CHUNK)
    row = lambda i: (i, 0)
    vec = lambda w: pl.BlockSpec((1, w), lambda i: (0, 0))
    tab = pl.BlockSpec((tm, LANES), row)
    return pl.pallas_call(
        _small_kernel,
        grid=(m // tm,),
        in_specs=[pl.BlockSpec((tm, SMALL_W), row), vec(MLA_Q_RANK), vec(MLA_KV_RANK), tab, tab, tab,
                  vec(LANES), vec(LANES)],
        out_specs=[pl.BlockSpec((tm, MLA_Q_RANK), row), pl.BlockSpec((tm, KV_LHS_W), row),
                   pl.BlockSpec((tm, LANES), row)],
        out_shape=[jax.ShapeDtypeStruct((m, MLA_Q_RANK), BF16), jax.ShapeDtypeStruct((m, KV_LHS_W), BF16),
                   jax.ShapeDtypeStruct((m, LANES), F32)],
        compiler_params=_cparams(("parallel",)),
        name="small_prologue",
    )(p_small, q_norm, kv_norm, ck, s1, s2, nexp, dtb)


def _attn_kernel(*refs, with_lat):
    if with_lat:
        q_ref, kc_ref, vc_ref, kl_ref, vl_ref, o_ref = refs
    else:
        q_ref, kc_ref, vc_ref, o_ref = refs
    nt = (((1,), (1,)), ((), ()))
    heads = range(ATTN_HEADS)
    ks = lambda hh: slice(hh * MLA_PAD, (hh + 1) * MLA_PAD)
    vs = lambda hh: slice(hh * MLA_V, (hh + 1) * MLA_V)
    q = [q_ref[:, ks(hh)] for hh in heads]
    sc = [lax.dot_general(q[hh], kc_ref[:, ks(hh)], nt, preferred_element_type=F32) for hh in heads]
    mx = [jnp.max(sc[hh], axis=-1, keepdims=True) for hh in heads]
    if with_lat:
        sl = [lax.dot_general(q[hh], kl_ref[:, ks(hh)], nt, preferred_element_type=F32) for hh in heads]
        mx = [jnp.maximum(mx[hh], jnp.max(sl[hh], axis=-1, keepdims=True)) for hh in heads]
    pc = [jnp.exp(sc[hh] - mx[hh]) for hh in heads]
    den = [jnp.sum(pc[hh], axis=-1, keepdims=True) for hh in heads]
    acc = [jnp.dot(pc[hh].astype(BF16), vc_ref[:, vs(hh)], preferred_element_type=F32) for hh in heads]
    if with_lat:
        pl_ = [jnp.exp(sl[hh] - mx[hh]) for hh in heads]
        den = [den[hh] + jnp.sum(pl_[hh], axis=-1, keepdims=True) for hh in heads]
        acc = [acc[hh] + jnp.dot(pl_[hh].astype(BF16), vl_ref[:, vs(hh)], preferred_element_type=F32)
               for hh in heads]
    for hh in heads:
        o_ref[:, vs(hh)] = (acc[hh] / den[hh]).astype(o_ref.dtype)


def _attention(q, kv, *, n_batch, t_lat, t_ctx, latent_queries, tq=256):
    n_lat = n_batch * t_lat
    kw = ATTN_HEADS * MLA_PAD
    vw = ATTN_HEADS * MLA_V
    vcol0 = MLA_HEADS * MLA_PAD // vw
    if latent_queries:
        tq = _tile(t_lat, tq, SUBLANES)
        nq = t_lat // tq
        m_out = n_lat
        qrow = lambda b, hh, qi: (b * nq + qi, hh)
        orow = qrow
    else:
        tq = t_ctx
        nq = 1
        m_out = n_batch * t_ctx
        qrow = lambda b, hh, qi: (n_lat // t_ctx + b, hh)
        orow = lambda b, hh, qi: (b, hh)
    ctx_blk = n_lat // t_ctx
    in_specs = [pl.BlockSpec((tq, kw), qrow),
                pl.BlockSpec((t_ctx, kw), lambda b, hh, qi: (ctx_blk + b, hh)),
                pl.BlockSpec((t_ctx, vw), lambda b, hh, qi: (ctx_blk + b, vcol0 + hh))]
    args = [q, kv, kv]
    if latent_queries:
        in_specs += [pl.BlockSpec((t_lat, kw), lambda b, hh, qi: (b, hh)),
                     pl.BlockSpec((t_lat, vw), lambda b, hh, qi: (b, vcol0 + hh))]
        args += [kv, kv]
    return pl.pallas_call(
        functools.partial(_attn_kernel, with_lat=latent_queries),
        grid=(n_batch, MLA_HEADS // ATTN_HEADS, nq),
        in_specs=in_specs,
        out_specs=pl.BlockSpec((tq, vw), orow),
        out_shape=jax.ShapeDtypeStruct((m_out, MLA_HEADS * MLA_V), BF16),
        compiler_params=_cparams(("parallel", "parallel", "arbitrary")),
        name="attention_lat" if latent_queries else "attention_ctx",
    )(*args)


def _gdn_feat_kernel(x_ref, xp_ref, xn_ref, w_ref, o_ref, *, tm, tc, n_lat, t_lat, t_ctx):
    start, end = _seq_edges(pl.program_id(0), tm, n_lat, t_lat, t_ctx)
    j = pl.program_id(1)
    prev = jnp.where(start, 0.0, xp_ref[HALO - 1:HALO, :])
    nxt = jnp.where(end, 0.0, xn_ref[0:1, :])
    f = _silu(_conv3(x_ref[...], prev, nxt, w_ref))
    is_q = j < GDN_QK // tc
    is_v = j >= 2 * GDN_QK // tc
    post = jnp.where(is_q, GDN_DK ** -0.5, 1.0)
    for hh in range(tc // GDN_DK):
        fh = f[:, hh * GDN_DK:(hh + 1) * GDN_DK]
        inv = lax.rsqrt(jnp.sum(fh * fh, axis=-1, keepdims=True) + EPS) * post
        o_ref[:, hh * GDN_DK:(hh + 1) * GDN_DK] = fh * jnp.where(is_v, 1.0, inv)


def _gdn_features(p_gdn, w, *, n_lat, t_lat, t_ctx, tm=256, tc=1024):
    n_rows = p_gdn.shape[0]
    width = 2 * GDN_QK + GDN_VW
    tc = _tile(GDN_QK, tc, GDN_DK)
    tm = _tile(math.gcd(t_lat, t_ctx), tm, HALO)
    xp, xn = _halo_specs(tm, tc, lambda j: j, n_rows)
    return pl.pallas_call(
        functools.partial(_gdn_feat_kernel, tm=tm, tc=tc, n_lat=n_lat, t_lat=t_lat, t_ctx=t_ctx),
        grid=(n_rows // tm, width // tc),
        in_specs=[pl.BlockSpec((tm, tc), lambda i, j: (i, j)), xp, xn,
                  pl.BlockSpec((3, tc), lambda i, j: (0, j))],
        out_specs=pl.BlockSpec((tm, tc), lambda i, j: (i, j)),
        out_shape=jax.ShapeDtypeStruct((n_rows, width), F32),
        compiler_params=_cparams(("parallel", "parallel")),
        name="gdn_features",
    )(p_gdn, p_gdn, p_gdn, w)


def _block_diag(xp, blk, n):
    return jnp.concatenate([jnp.where(blk == p, xp, jnp.zeros_like(xp)) for p in range(n)], axis=0)


def _gdn_scan_kernel(fq0, fk0, fv0, gb0, gr0, fq1, fk1, fv1, gb1, gr1, o0_ref, o1_ref, state_ref):
    c = GDN_CHUNK
    pw = QUAD * c
    nt = (((1,), (1,)), ((), ()))
    tn = (((0,), (0,)), ((), ()))

    @pl.when(pl.program_id(1) == 0)
    def _():
        state_ref[...] = jnp.zeros_like(state_ref)

    ri = lax.broadcasted_iota(jnp.int32, (c, pw), 0)
    ci = lax.broadcasted_iota(jnp.int32, (c, pw), 1)
    cj = ci % c
    blk = ci // c
    blk_k = lax.broadcasted_iota(jnp.int32, (c, QUAD * GDN_DK), 1) // GDN_DK
    eye_p = jnp.where(ri == cj, 1.0, 0.0)
    lanes = lax.broadcasted_iota(jnp.int32, (c, LANES), 1)
    dirs = ((fq0, fk0, fv0, gb0, gr0), (fq1, fk1, fv1, gb1, gr1))
    nq = GDN_HEADS // QUAD

    quads = []
    for d, (fq, fk, fv, gb_ref, gr_ref) in enumerate(dirs):
        incl = (ri <= cj) if d == 1 else (ri >= cj)
        strict = (ri < cj) if d == 1 else (ri > cj)
        gbv = gb_ref[...]
        for g in range(nq):
            cols = slice(g * QUAD * GDN_DK, (g + 1) * QUAD * GDN_DK)
            k4, q4, v4 = fk[:, cols], fq[:, cols], fv[:, cols]
            grow = gr_ref[d * nq + g:d * nq + g + 1, :]
            heads, kb_l, gcol_p = [], [], None
            for p in range(QUAD):
                h = g * QUAD + p
                hs = slice(p * GDN_DK, (p + 1) * GDN_DK)
                gcol = jnp.sum(jnp.where(lanes == 64 + d * GDN_HEADS + h, gbv, 0.0), axis=1, keepdims=True)
                bcol = jnp.sum(jnp.where(lanes == 96 + d * GDN_HEADS + h, gbv, 0.0), axis=1, keepdims=True)
                gend = grow[:, p * c:p * c + 1] if d == 1 else grow[:, (p + 1) * c - 1:(p + 1) * c]
                egc = jnp.exp(gcol)
                kp = k4[:, hs]
                kb = kp * bcol
                kb_l.append(kb)
                heads.append(dict(
                    h=h,
                    rhs=jnp.concatenate([v4[:, hs] * bcol, kb * egc], axis=1).astype(BF16),
                    qe=q4[:, hs] * egc,
                    kt=(kp * jnp.exp(gend - gcol)).astype(BF16),
                    cd=jnp.exp(gend),
                ))
                gcol_p = gcol if gcol_p is None else jnp.where(blk == p, gcol, gcol_p)
            decay = jnp.exp(jnp.where(incl, gcol_p - grow, -jnp.inf))
            kbd = _block_diag(k4.astype(BF16), blk_k, QUAD)
            lhs = jnp.concatenate([jnp.concatenate(kb_l, axis=1), q4], axis=0).astype(BF16)
            quads.append(dict(d=d, heads=heads, decay=decay, strict=strict, kbd=kbd, lhs=lhs))

    for qd in quads:
        sc = lax.dot_general(qd["lhs"], qd["kbd"], nt, preferred_element_type=F32)
        qd["x"] = -jnp.where(qd["strict"], sc[:c] * qd["decay"], 0.0)
        qd["intra"] = (sc[c:] * qd["decay"]).astype(BF16)
        qd["tinv"] = eye_p + qd["x"]

    def split(v):
        hi = v.astype(BF16)
        return hi, (v - hi.astype(F32)).astype(BF16)

    def times_bd(v, bd_hi, bd_lo):
        hi, lo = split(v)
        r = jnp.dot(jnp.concatenate([hi, lo], axis=0), bd_hi, preferred_element_type=F32)
        return r[:c] + r[c:] + jnp.dot(hi, bd_lo, preferred_element_type=F32)

    def bd_split(v):
        hi, lo = split(v)
        return _block_diag(hi, blk, QUAD), _block_diag(lo, blk, QUAD)

    for qd in quads:
        qd["bd"] = bd_split(qd["x"])
    step = 2
    while step < c:
        for qd in quads:
            qd["x"] = times_bd(qd["x"], *qd["bd"])
        for qd in quads:
            qd["bd"] = bd_split(qd["x"])
        for qd in quads:
            qd["tinv"] = qd["tinv"] + times_bd(qd["tinv"], *qd["bd"])
        step *= 2

    for qd in quads:
        t16 = qd["tinv"].astype(BF16)
        for p, hd in enumerate(qd["heads"]):
            hd["uw"] = jnp.dot(t16[:, p * c:(p + 1) * c], hd["rhs"], preferred_element_type=F32)

    for qd in quads:
        for hd in qd["heads"]:
            hd["st"] = state_ref[qd["d"], hd["h"]]
            lhs = jnp.concatenate([hd["uw"][:, GDN_DV:], hd["qe"]], axis=0).astype(BF16)
            hd["ws"] = jnp.dot(lhs, hd["st"].astype(BF16), preferred_element_type=F32)
    for qd in quads:
        for hd in qd["heads"]:
            hd["vn"] = (hd["uw"][:, :GDN_DV] - hd["ws"][:c]).astype(BF16)
    outs = ([], [])
    for qd in quads:
        for p, hd in enumerate(qd["heads"]):
            outs[qd["d"]].append(hd["ws"][c:] + jnp.dot(qd["intra"][:, p * c:(p + 1) * c], hd["vn"],
                                                         preferred_element_type=F32))
            state_ref[qd["d"], hd["h"]] = hd["st"] * hd["cd"] + lax.dot_general(
                hd["kt"], hd["vn"], tn, preferred_element_type=F32)
    o0_ref[...] = jnp.concatenate(outs[0], axis=1)
    o1_ref[...] = jnp.concatenate(outs[1], axis=1)


def _gdn_scan(feat, gb, grows, *, n_batch, t_lat, t_ctx):
    n_rows = feat.shape[0]
    c = GDN_CHUNK
    n_lat = n_batch * t_lat
    nc_ctx, nc_lat = t_ctx // c, t_lat // c

    def chunk_row(d):
        def f(b, s):
            if d == 0:
                return jnp.where(s < nc_ctx, n_lat // c + b * nc_ctx + s, b * nc_lat + (s - nc_ctx))
            return jnp.where(s < nc_ctx, n_lat // c + b * nc_ctx + (nc_ctx - 1 - s),
                             b * nc_lat + (nc_lat - 1 - (s - nc_ctx)))
        return f

    in_specs, args = [], []
    for d in range(N_DIR):
        cr = chunk_row(d)
        for col0 in range(3):
            in_specs.append(pl.BlockSpec((c, GDN_QK), lambda b, s, cr=cr, col0=col0: (cr(b, s), col0)))
            args.append(feat)
        in_specs.append(pl.BlockSpec((c, LANES), lambda b, s, cr=cr: (cr(b, s), 0)))
        args.append(gb)
        in_specs.append(pl.BlockSpec((None,) + grows.shape[1:], lambda b, s, cr=cr: (cr(b, s), 0, 0)))
        args.append(grows)
    out_specs = [pl.BlockSpec((c, GDN_VW), lambda b, s, cr=chunk_row(d): (cr(b, s), 0)) for d in range(N_DIR)]
    return pl.pallas_call(
        _gdn_scan_kernel,
        grid=(n_batch, nc_ctx + nc_lat),
        in_specs=in_specs, out_specs=out_specs,
        out_shape=[jax.ShapeDtypeStruct((n_rows, GDN_VW), F32)] * N_DIR,
        scratch_shapes=[pltpu.VMEM((N_DIR, GDN_HEADS, GDN_DK, GDN_DV), F32)],
        compiler_params=_cparams(("parallel", "arbitrary")),
        name="gdn_scan",
    )(*args)


def _pack_decay_rows(gb):
    n_chunks = gb.shape[0] // GDN_CHUNK
    g = gb[:, 64:96].reshape(n_chunks, GDN_CHUNK, N_DIR, GDN_HEADS // QUAD, QUAD)
    return jnp.transpose(g, (0, 2, 3, 4, 1)).reshape(n_chunks, N_DIR * GDN_HEADS // QUAD, QUAD * GDN_CHUNK)


def _gdn_out_kernel(o0_ref, o1_ref, z_ref, w_ref, y_ref):
    o = o0_ref[...] + o1_ref[...]
    z = z_ref[...]
    for hh in range(o.shape[1] // GDN_DV):
        sl = slice(hh * GDN_DV, (hh + 1) * GDN_DV)
        oh = o[:, sl]
        yh = oh * lax.rsqrt(jnp.mean(oh * oh, axis=-1, keepdims=True) + EPS) * w_ref[...]
        y_ref[:, sl] = (yh * _silu(z[:, sl])).astype(y_ref.dtype)


def _gdn_output(o0, o1, p_gdn, w, *, m, tm=256, tc=1024):
    tc = _tile(GDN_VW, tc, GDN_DV)
    tm = _tile(m, tm, 16)
    z0 = (2 * GDN_QK + GDN_VW) // tc
    blk = pl.BlockSpec((tm, tc), lambda i, j: (i, j))
    return pl.pallas_call(
        _gdn_out_kernel,
        grid=(m // tm, GDN_VW // tc),
        in_specs=[blk, blk, pl.BlockSpec((tm, tc), lambda i, j: (i, z0 + j)),
                  pl.BlockSpec((1, GDN_DV), lambda i, j: (0, 0))],
        out_specs=blk,
        out_shape=jax.ShapeDtypeStruct((m, GDN_VW), BF16),
        compiler_params=_cparams(("parallel", "parallel")),
        name="gdn_output",
    )(o0, o1, p_gdn, w)


def _merge_kernel(a0, a1, a2, w_ref, g0, g1, g2, o_ref):
    acc = g0[...].astype(F32) * jnp.dot(a0[...], w_ref[0], preferred_element_type=F32)
    acc += g1[...].astype(F32) * jnp.dot(a1[...], w_ref[1], preferred_element_type=F32)
    acc += g2[...].astype(F32) * jnp.dot(a2[...], w_ref[2], preferred_element_type=F32)
    o_ref[...] = acc.astype(o_ref.dtype)


def _merge(attn, conv, gdn, w_branch, gates, *, m, tm=512, tn=512):
    d = w_branch.shape[-1]
    tm = _tile(m, tm, 16)
    tn = _tile(d, tn, LANES)
    nb = d // tn
    lhs = pl.BlockSpec((tm, BRANCH_W), lambda j, i: (i, 0))
    gate = lambda n: pl.BlockSpec((tm, tn), lambda j, i: (i, n * nb + j))
    return pl.pallas_call(
        _merge_kernel,
        grid=(nb, m // tm),
        in_specs=[lhs, lhs, lhs, pl.BlockSpec((N_BRANCH, BRANCH_W, tn), lambda j, i: (0, 0, j)),
                  gate(0), gate(1), gate(2)],
        out_specs=pl.BlockSpec((tm, tn), lambda j, i: (i, j)),
        out_shape=jax.ShapeDtypeStruct((m, d), BF16),
        compiler_params=_cparams(("parallel", "parallel")),
        name="merge",
    )(attn, conv, gdn, w_branch, gates, gates, gates)


def _route_kernel(lg_ref, bias_ref, idx_ref, rank_ref, wt_ref, cnt_ref, run_ref):
    @pl.when(pl.program_id(0) == 0)
    def _():
        run_ref[...] = jnp.zeros_like(run_ref)

    aff = jax.nn.sigmoid(lg_ref[...])
    sel = aff + bias_ref[...]
    e, tn = aff.shape
    eidx = lax.broadcasted_iota(jnp.int32, (e, tn), 0)
    best = None
    for g in range(N_GROUPS):
        r = [sel[g * EXPERTS_PER_GROUP + t:g * EXPERTS_PER_GROUP + t + 1, :] for t in range(EXPERTS_PER_GROUP)]
        top2 = None
        for a in range(EXPERTS_PER_GROUP):
            for b in range(a + 1, EXPERTS_PER_GROUP):
                s = r[a] + r[b]
                top2 = s if top2 is None else jnp.maximum(top2, s)
        if best is None:
            best, gidx = top2, jnp.zeros_like(top2, dtype=jnp.int32)
        else:
            better = top2 > best
            best = jnp.where(better, top2, best)
            gidx = jnp.where(better, g, gidx)
    masked = jnp.where(eidx // EXPERTS_PER_GROUP == gidx, sel, -jnp.inf)
    m1 = jnp.max(masked, axis=0, keepdims=True)
    i1 = jnp.min(jnp.where(masked == m1, eidx, e), axis=0, keepdims=True)
    masked2 = jnp.where(eidx == i1, -jnp.inf, masked)
    m2 = jnp.max(masked2, axis=0, keepdims=True)
    i2 = jnp.min(jnp.where(masked2 == m2, eidx, e), axis=0, keepdims=True)
    hot1 = eidx == i1
    hot2 = eidx == i2
    w1 = jnp.sum(jnp.where(hot1, aff, 0.0), axis=0, keepdims=True)
    w2 = jnp.sum(jnp.where(hot2, aff, 0.0), axis=0, keepdims=True)
    den = w1 + w2
    hot = jnp.where(hot1 | hot2, 1.0, 0.0)
    ra = lax.broadcasted_iota(jnp.int32, (tn, tn), 0)
    rb = lax.broadcasted_iota(jnp.int32, (tn, tn), 1)
    upper = jnp.where(ra <= rb, 1.0, 0.0).astype(BF16)
    incl = jnp.dot(hot.astype(BF16), upper, preferred_element_type=F32)
    pos = run_ref[...] + incl - hot
    rank1 = jnp.sum(jnp.where(hot1, pos, 0.0), axis=0, keepdims=True)
    rank2 = jnp.sum(jnp.where(hot2, pos, 0.0), axis=0, keepdims=True)
    run_ref[...] = run_ref[...] + incl[:, tn - 1:tn]
    idx_ref[...] = jnp.concatenate([i1, i2], axis=0)
    rank_ref[...] = jnp.concatenate([rank1, rank2], axis=0).astype(jnp.int32)
    wrow = jnp.concatenate([w1 / den, w2 / den, jnp.zeros((LANES - TOP_K, tn), F32)], axis=0)
    wt_ref[...] = wrow.T
    cnt_ref[...] = run_ref[...].astype(jnp.int32)


def _route(logits, bias_col, *, tn=512):
    e, m = logits.shape
    tn = _tile(m, tn, LANES)
    row2 = pl.BlockSpec((TOP_K, tn), lambda i: (0, i))
    return pl.pallas_call(
        _route_kernel,
        grid=(m // tn,),
        in_specs=[pl.BlockSpec((e, tn), lambda i: (0, i)), pl.BlockSpec((e, 1), lambda i: (0, 0))],
        out_specs=[row2, row2, pl.BlockSpec((tn, LANES), lambda i: (i, 0)), pl.BlockSpec((e, 1), lambda i: (0, 0))],
        out_shape=[jax.ShapeDtypeStruct((TOP_K, m), jnp.int32), jax.ShapeDtypeStruct((TOP_K, m), jnp.int32),
                   jax.ShapeDtypeStruct((m, LANES), F32), jax.ShapeDtypeStruct((e, 1), jnp.int32)],
        scratch_shapes=[pltpu.VMEM((e, 1), F32)],
        compiler_params=_cparams(("arbitrary",)),
        name="route",
    )(logits, bias_col)


def _moe_plan(counts, n_tokens):
    cnt = counts.reshape(-1)
    gsz = ((cnt + MOE_TM - 1) // MOE_TM) * MOE_TM
    gend = jnp.cumsum(gsz)
    gstart = gend - gsz
    nt = (TOP_K * n_tokens + MOE_TM - 1) // MOE_TM + N_EXPERTS
    used = (gend[-1] // MOE_TM).astype(jnp.int32)
    tile0 = jnp.arange(nt, dtype=jnp.int32) * MOE_TM
    te = jnp.minimum(jnp.sum(gend[None, :] <= tile0[:, None], axis=1), N_EXPERTS - 1).astype(jnp.int32)
    te = jnp.where(jnp.arange(nt) < used, te, te[jnp.maximum(used - 1, 0)])
    pad1 = gend.at[-1].set(nt * MOE_TM)
    return dict(gstart=gstart.astype(jnp.int32).reshape(-1, 1), pad0=(gstart + cnt).astype(jnp.int32),
                pad1=pad1.astype(jnp.int32), tile_expert=te, used=used.reshape(1), n_rows=nt * MOE_TM)


def _dest_kernel(idx_ref, rank_ref, start_ref, dest_ref):
    idx = idx_ref[...]
    out = rank_ref[...]
    for ex in range(N_EXPERTS):
        out = out + jnp.where(idx == ex, start_ref[ex:ex + 1, :], 0)
    dest_ref[...] = out


def _dest(idx, rank, gstart_col, *, tn=512):
    k, m = idx.shape
    tn = _tile(m, tn, LANES)
    row = pl.BlockSpec((k, tn), lambda i: (0, i))
    return pl.pallas_call(
        _dest_kernel,
        grid=(m // tn,),
        in_specs=[row, row, pl.BlockSpec((N_EXPERTS, 1), lambda i: (0, 0))],
        out_specs=row,
        out_shape=jax.ShapeDtypeStruct((k, m), jnp.int32),
        compiler_params=_cparams(("parallel",)),
        name="moe_dest",
    )(idx, rank, gstart_col)


def _dispatch_kernel(d0_ref, d1_ref, pad0_ref, pad1_ref, h_ref, xs_ref, sem, *, tm):
    row0 = pl.program_id(0) * tm

    def row_copy(src_row, dst_row):
        return pltpu.make_async_copy(h_ref.at[pl.ds(src_row, 1)], xs_ref.at[pl.ds(dst_row, 1)], sem)

    def wait_rows(n):
        def body(r, carry):
            row_copy(0, 0).wait()
            return carry
        lax.fori_loop(0, n, body, 0)

    def scatter_token(r, carry):
        row_copy(r, d0_ref[row0 + r]).start()
        row_copy(r, d1_ref[row0 + r]).start()
        return carry

    lax.fori_loop(0, tm, scatter_token, 0)
    wait_rows(TOP_K * tm)

    @pl.when(pl.program_id(0) == 0)
    def _():
        for ex in range(N_EXPERTS):
            lo, hi = pad0_ref[ex], pad1_ref[ex]

            def fill(r, carry, lo=lo):
                row_copy(0, r).start()

                @pl.when(r - lo >= DMA_WINDOW)
                def _():
                    wait_rows(1)
                return carry

            lax.fori_loop(lo, hi, fill, 0)
            wait_rows(jnp.minimum(hi - lo, DMA_WINDOW))


def _dispatch(h, dest0, dest1, pad0, pad1, *, n_rows, tm=256):
    m, d = h.shape
    tm = _tile(m, tm, SUBLANES)
    return pl.pallas_call(
        functools.partial(_dispatch_kernel, tm=tm),
        grid_spec=pltpu.PrefetchScalarGridSpec(
            num_scalar_prefetch=4, grid=(m // tm,),
            in_specs=[pl.BlockSpec((tm, d), lambda i, *_: (i, 0))],
            out_specs=pl.BlockSpec(memory_space=pl.ANY),
            scratch_shapes=[pltpu.SemaphoreType.DMA(())]),
        out_shape=jax.ShapeDtypeStruct((n_rows, d), h.dtype),
        compiler_params=_cparams(("arbitrary",)),
        name="moe_dispatch",
    )(dest0, dest1, pad0, pad1, h)


def _new_expert_weights(te_ref):
    i = pl.program_id(1)
    return jnp.logical_or(i == 0, te_ref[i] != te_ref[jnp.maximum(i - 1, 0)])


def _expert_up_kernel(te_ref, used_ref, x_ref, wg_ref, wu_ref, o_ref, wg16, wu16):
    active = pl.program_id(1) < used_ref[0]

    @pl.when(_new_expert_weights(te_ref))
    def _():
        wg16[...] = wg_ref[...].astype(BF16)
        wu16[...] = wu_ref[...].astype(BF16)

    @pl.when(active)
    def _():
        x = x_ref[...].astype(BF16)
        g = jnp.dot(x, wg16[...], preferred_element_type=F32)
        u = jnp.dot(x, wu16[...], preferred_element_type=F32)
        o_ref[...] = (_silu(g) * u).astype(o_ref.dtype)

    @pl.when(jnp.logical_not(active))
    def _():
        o_ref[...] = jnp.zeros_like(o_ref)


def _expert_up(xs, wg, wu, layer, tile_expert, used, *, tn=512):
    r, d = xs.shape
    f = wg.shape[-1]
    tn = _tile(f, tn, LANES)
    wspec = pl.BlockSpec((None, None, d, tn), lambda j, i, te, us: (layer, te[i], 0, j))
    return pl.pallas_call(
        _expert_up_kernel,
        grid_spec=pltpu.PrefetchScalarGridSpec(
            num_scalar_prefetch=2, grid=(f // tn, r // MOE_TM),
            in_specs=[pl.BlockSpec((MOE_TM, d), lambda j, i, te, us: (i, 0)), wspec, wspec],
            out_specs=pl.BlockSpec((MOE_TM, tn), lambda j, i, te, us: (i, j)),
            scratch_shapes=[pltpu.VMEM((d, tn), BF16), pltpu.VMEM((d, tn), BF16)]),
        out_shape=jax.ShapeDtypeStruct((r, f), BF16),
        compiler_params=_cparams(("arbitrary", "arbitrary")),
        name="moe_up",
    )(tile_expert, used, xs, wg, wu)


def _expert_down_kernel(te_ref, used_ref, a_ref, wd_ref, o_ref, wd16):
    active = pl.program_id(1) < used_ref[0]

    @pl.when(_new_expert_weights(te_ref))
    def _():
        wd16[...] = wd_ref[...].astype(BF16)

    @pl.when(active)
    def _():
        o_ref[...] = jnp.dot(a_ref[...], wd16[...], preferred_element_type=F32)

    @pl.when(jnp.logical_not(active))
    def _():
        o_ref[...] = jnp.zeros_like(o_ref)


def _expert_down(act, wd, layer, tile_expert, used, *, tn=2048):
    r, f = act.shape
    d = wd.shape[-1]
    tn = _tile(d, tn, LANES)
    return pl.pallas_call(
        _expert_down_kernel,
        grid_spec=pltpu.PrefetchScalarGridSpec(
            num_scalar_prefetch=2, grid=(d // tn, r // MOE_TM),
            in_specs=[pl.BlockSpec((MOE_TM, f), lambda j, i, te, us: (i, 0)),
                      pl.BlockSpec((None, None, f, tn), lambda j, i, te, us: (layer, te[i], 0, j))],
            out_specs=pl.BlockSpec((MOE_TM, tn), lambda j, i, te, us: (i, j)),
            scratch_shapes=[pltpu.VMEM((f, tn), BF16)]),
        out_shape=jax.ShapeDtypeStruct((r, d), F32),
        compiler_params=_cparams(("arbitrary", "arbitrary")),
        name="moe_down",
    )(tile_expert, used, act, wd)


def _rope_tables(n_batch, t_lat, t_ctx, width, lane0, scale):
    nf = MLA_ROPE // 4
    pos = jnp.arange(t_lat)
    inv = ROPE_THETA ** (-jnp.arange(nf, dtype=F32) / nf)
    ang_r = (pos // GRID_W).astype(F32)[:, None] * inv
    ang_c = (pos % GRID_W).astype(F32)[:, None] * inv
    z = jnp.zeros((t_lat, nf), F32)
    cos = jnp.concatenate([jnp.cos(ang_r)] * 2 + [jnp.cos(ang_c)] * 2, axis=1)
    s1 = jnp.concatenate([z, jnp.sin(ang_r), z, jnp.sin(ang_c)], axis=1)
    s2 = jnp.concatenate([-jnp.sin(ang_r), z, -jnp.sin(ang_c), z], axis=1)

    def place(rot, ctx_rot, fill):
        def rows(r, n):
            left = jnp.full((n, lane0), fill, F32)
            right = jnp.zeros((n, width - lane0 - MLA_ROPE), F32)
            return jnp.concatenate([left, r, right], axis=1)
        lat = rows(rot, t_lat)
        ctx = rows(jnp.broadcast_to(ctx_rot, (t_ctx, MLA_ROPE)), t_ctx)
        return jnp.concatenate([lat] * n_batch + [ctx] * n_batch, axis=0) * scale

    return (place(cos, jnp.ones((MLA_ROPE,), F32), 1.0), place(s1, jnp.zeros((MLA_ROPE,), F32), 0.0),
            place(s2, jnp.zeros((MLA_ROPE,), F32), 0.0))


def _prep_w_in(w):
    mla = MLA_Q_RANK + MLA_KV_RANK + MLA_ROPE
    mid = 3 * SC_WIDTH + 2 * GDN_QK + 2 * GDN_VW
    ab = 2 * N_DIR * GDN_HEADS
    gates0 = mla + mid + ab
    w16 = jnp.concatenate([w[:, mla:mla + mid], w[:, gates0:], w[:, :mla], w[:, mla + mid:gates0]],
                          axis=1).astype(BF16)
    n_gates = w.shape[1] - gates0
    cols = dict(sc=(0, 3 * SC_WIDTH), gdn=(3 * SC_WIDTH, 2 * GDN_QK + 2 * GDN_VW), gates=(mid, n_gates),
                small=(mid + n_gates, SMALL_W))
    return w16, cols


def _prep_w_uq(w):
    r = w.shape[0]
    w3 = w.reshape(r, MLA_HEADS, MLA_QK)
    w3 = jnp.concatenate([w3, jnp.zeros((r, MLA_HEADS, MLA_PAD - MLA_QK), w.dtype)], axis=-1)
    return w3.reshape(r, MLA_HEADS * MLA_PAD).astype(BF16)


def _prep_w_ukv(w):
    r = w.shape[0]
    w3 = w.reshape(r, MLA_HEADS, MLA_NOPE + MLA_V)
    k_top = jnp.concatenate([w3[..., :MLA_NOPE], jnp.zeros((r, MLA_HEADS, MLA_PAD - MLA_NOPE), w.dtype)], -1)
    v_top = w3[..., MLA_NOPE:]
    eye = jnp.eye(MLA_ROPE, dtype=w.dtype)[:, None, :]
    k_rope = jnp.concatenate([jnp.zeros((MLA_ROPE, MLA_HEADS, MLA_NOPE), w.dtype),
                              jnp.broadcast_to(eye, (MLA_ROPE, MLA_HEADS, MLA_ROPE)),
                              jnp.zeros((MLA_ROPE, MLA_HEADS, MLA_PAD - MLA_QK), w.dtype)], -1)
    pad_rows = KV_LHS_W - r - MLA_ROPE
    keys = jnp.concatenate([k_top.reshape(r, -1), k_rope.reshape(MLA_ROPE, -1),
                            jnp.zeros((pad_rows, MLA_HEADS * MLA_PAD), w.dtype)], 0)
    vals = jnp.concatenate([v_top.reshape(r, -1), jnp.zeros((MLA_ROPE + pad_rows, MLA_HEADS * MLA_V), w.dtype)], 0)
    return jnp.concatenate([keys, vals], axis=1).astype(BF16)


def kernel(x, c, ctx, c_ctx, w_ada, b_ada, w_in, q_norm, kv_norm, w_uq, w_ukv, sc_conv, gdn_conv, gdn_a_log,
           gdn_dt_bias, gdn_norm, w_branch, w_o, ln1_g, ln1_b, ln2_g, ln2_b, w_router, router_bias, w_e_gate,
           w_e_up, w_e_down):
    n_batch, t_lat, d = x.shape
    t_ctx = ctx.shape[1]
    depth = w_ada.shape[0]
    n_lat = n_batch * t_lat
    n_all = n_lat + n_batch * t_ctx
    alpha = (2 * depth) ** 0.25
    geo = dict(t_lat=t_lat, n_batch=n_batch)

    xa = jnp.concatenate([x.reshape(n_lat, d), ctx.reshape(n_batch * t_ctx, d)], axis=0)
    cc = jnp.zeros((SUBLANES, d), F32).at[:n_batch].set(c).at[n_batch].set(c_ctx)
    mods = [
        _matmul(cc, w_ada, b_layer=l, out_dtype=F32, tm=SUBLANES, tn=512, lhs_silu=True,
                extra=((b_ada[l][None, :], (1, 512), lambda i, j: (0, j)),), epilogue=_ep_bias,
                name="adaln").reshape(SUBLANES, ADA_MULT, d)
        for l in range(depth)
    ]
    q_tabs = _rope_tables(n_batch, t_lat, t_ctx, MLA_PAD, MLA_NOPE, MLA_QK ** -0.5)
    k_tabs = _rope_tables(n_batch, t_lat, t_ctx, LANES, 0, 1.0)
    w_router_t = w_router.T.astype(BF16)
    bias_col = router_bias.reshape(-1, 1)
    lane_vec = lambda v: jnp.zeros((1, LANES), F32).at[0, 64:96].set(v.reshape(-1))

    h1 = _modulate(xa, mods[0], shift=0, scale=1, **geo)
    for l in range(depth):
        with_ctx = l < depth - 1
        m = n_all if with_ctx else n_lat
        w16, cols = _prep_w_in(w_in[l])

        wide = dict(tn=1024, cols_outer=True)
        p_small = _matmul(h1, w16, cols=cols["small"], out_dtype=F32, tn=SMALL_W, name="in_small")
        p_sc = _matmul(h1, w16, cols=cols["sc"], out_dtype=F32, m=m, name="in_sc", **wide)
        p_gdn = _matmul(h1, w16, cols=cols["gdn"], out_dtype=F32, name="in_gdn", **wide)
        gates = _matmul(h1, w16, cols=cols["gates"], out_dtype=BF16, m=m, epilogue=_ep_sigmoid, name="in_gates",
                        **wide)

        cqn, kvl, gb = _small_prologue(p_small, q_norm[l][None, :], kv_norm[l][None, :], *k_tabs,
                                       lane_vec(-jnp.exp(gdn_a_log[l])), lane_vec(gdn_dt_bias[l]))
        q = _matmul(cqn, _prep_w_uq(w_uq[l]), out_dtype=BF16, m=m, tn=4 * MLA_PAD, epilogue=_ep_rope,
                    extra=tuple((t, (_tile(m, 512, SUBLANES), MLA_PAD), lambda i, j: (i, 0)) for t in q_tabs),
                    name="q_up")
        kv = _matmul(kvl, _prep_w_ukv(w_ukv[l]), out_dtype=BF16, tn=2048, name="kv_up")
        attn = _attention(q, kv, n_batch=n_batch, t_lat=t_lat, t_ctx=t_ctx, latent_queries=True)
        if with_ctx:
            attn_ctx = _attention(q, kv, n_batch=n_batch, t_lat=t_lat, t_ctx=t_ctx, latent_queries=False)
            attn = jnp.concatenate([attn, attn_ctx], axis=0)

        conv = _short_conv(p_sc, sc_conv[l], m=m, n_lat=n_lat, t_lat=t_lat, t_ctx=t_ctx)

        feat = _gdn_features(p_gdn, gdn_conv[l], n_lat=n_lat, t_lat=t_lat, t_ctx=t_ctx)
        o0, o1 = _gdn_scan(feat, gb, _pack_decay_rows(gb), n_batch=n_batch, t_lat=t_lat, t_ctx=t_ctx)
        gdn = _gdn_output(o0, o1, p_gdn, gdn_norm[l][None, :], m=m)

        merged = _merge(attn, conv, gdn, w_branch[l].astype(BF16), gates, m=m)
        mix = _matmul(merged, w_o[l].astype(BF16), out_dtype=F32, name="w_o", **wide)
        x1, h2, logits = _ln_router(xa, mix, mods[l], ln1_g[l][None, :], ln1_b[l][None, :], w_router_t, m=m,
                                    alpha=alpha, **geo)

        idx, rank, wt, counts = _route(logits, bias_col)
        plan = _moe_plan(counts, m)
        dest = _dest(idx, rank, plan["gstart"])
        xs = _dispatch(h2, dest[0], dest[1], plan["pad0"], plan["pad1"], n_rows=plan["n_rows"])
        act = _expert_up(xs, w_e_gate, w_e_up, l, plan["tile_expert"], plan["used"])
        ys = _expert_down(act, w_e_down, l, plan["tile_expert"], plan["used"])
        res = _ln_combine(x1, ys, dest[0], dest[1], wt, mods[l], ln2_g[l][None, :], ln2_b[l][None, :], m=m,
                          alpha=alpha, mod_next=mods[l + 1] if with_ctx else None, **geo)
        xa = res[0]
        if with_ctx:
            h1 = res[1]
    return xa[:n_lat].reshape(n_batch, t_lat, d)
```

```python
import functools
import math

import jax
import jax.numpy as jnp
from jax import lax
from jax.experimental import pallas as pl
from jax.experimental.pallas import tpu as pltpu

F32 = jnp.float32
BF16 = jnp.bfloat16

GRID_W = 64
MLA_HEADS = 16
MLA_Q_RANK = 1024
MLA_KV_RANK = 512
MLA_NOPE = 128
MLA_ROPE = 64
MLA_V = 128
MLA_QK = MLA_NOPE + MLA_ROPE
ATTN_HEADS = 2
MLA_PAD = 256
ROPE_THETA = 10000.0
SC_WIDTH = 2048
GDN_HEADS = 16
GDN_DK = 128
GDN_DV = 128
GDN_QK = GDN_HEADS * GDN_DK
GDN_VW = GDN_HEADS * GDN_DV
GDN_CHUNK = 64
N_DIR = 2
QUAD = 4
BRANCH_W = 2048
N_BRANCH = 3
ADA_MULT = 6
N_EXPERTS = 16
N_GROUPS = 4
EXPERTS_PER_GROUP = N_EXPERTS // N_GROUPS
TOP_K = 2
EPS = 1e-6
SMALL_W = MLA_Q_RANK + MLA_KV_RANK + 128
KV_LHS_W = MLA_KV_RANK + 128

LANES = 128
SUBLANES = 8
VMEM_LIMIT = 56 * 1024 * 1024
MOE_TM = 256
DMA_WINDOW = 256
HALO = SUBLANES


def _cparams(sem):
    return pltpu.CompilerParams(dimension_semantics=sem, vmem_limit_bytes=VMEM_LIMIT)


def _tile(dim, pref, align):
    if dim <= pref:
        return dim
    t = (pref // align) * align
    while t >= align:
        if dim % t == 0:
            return t
        t -= align
    return dim


def _silu(v):
    return v * jax.nn.sigmoid(v)


def _mm_kernel(*refs, nk, n_extra, epilogue, lhs_silu):
    a_ref, b_ref = refs[0], refs[1]
    extra = refs[2:2 + n_extra]
    o_ref = refs[2 + n_extra]
    a = a_ref[...]
    if lhs_silu:
        a = _silu(a.astype(F32))
    part = jnp.dot(a.astype(BF16), b_ref[...].astype(BF16), preferred_element_type=F32)
    if nk == 1:
        o_ref[...] = epilogue(part, *extra).astype(o_ref.dtype)
        return
    acc_ref = refs[3 + n_extra]
    k = pl.program_id(2)

    @pl.when(k == 0)
    def _():
        acc_ref[...] = part

    @pl.when(k > 0)
    def _():
        acc_ref[...] += part

    @pl.when(k == nk - 1)
    def _():
        o_ref[...] = epilogue(acc_ref[...], *extra).astype(o_ref.dtype)


def _matmul(a, b, *, out_dtype, m=None, tm=512, tn=512, tk=None, b_layer=None, extra=(),
            epilogue=None, lhs_silu=False, cols_outer=False, name="matmul"):
    m = a.shape[0] if m is None else m
    kdim = a.shape[1]
    n = b.shape[-1]
    tm = _tile(m, tm, SUBLANES)
    tn = _tile(n, tn, LANES)
    tk = kdim if tk is None else _tile(kdim, tk, LANES)
    nk = kdim // tk
    if epilogue is None:
        epilogue = lambda acc: acc
    ij = (lambda g0, g1: (g1, g0)) if cols_outer else (lambda g0, g1: (g0, g1))
    grid = (n // tn, m // tm, nk) if cols_outer else (m // tm, n // tn, nk)
    a_spec = pl.BlockSpec((tm, tk), lambda g0, g1, k: (ij(g0, g1)[0], k))
    if b.ndim == 3:
        b_spec = pl.BlockSpec((None, tk, tn), lambda g0, g1, k: (b_layer, k, ij(g0, g1)[1]))
    else:
        b_spec = pl.BlockSpec((tk, tn), lambda g0, g1, k: (k, ij(g0, g1)[1]))
    extra_specs = [pl.BlockSpec(bs, (lambda g0, g1, k, im=im: im(*ij(g0, g1)))) for _, bs, im in extra]
    scratch = [pltpu.VMEM((tm, tn), F32)] if nk > 1 else []
    return pl.pallas_call(
        functools.partial(_mm_kernel, nk=nk, n_extra=len(extra), epilogue=epilogue, lhs_silu=lhs_silu),
        grid=grid,
        in_specs=[a_spec, b_spec] + extra_specs,
        out_specs=pl.BlockSpec((tm, tn), lambda g0, g1, k: ij(g0, g1)),
        out_shape=jax.ShapeDtypeStruct((m, n), out_dtype),
        scratch_shapes=scratch,
        compiler_params=_cparams(("parallel", "parallel", "arbitrary")),
        name=name,
    )(a, b, *[e[0] for e in extra])


def _ep_bias(acc, bias_ref):
    return acc + bias_ref[...]


def _ep_sigmoid(acc):
    return jax.nn.sigmoid(acc)


def _ep_rope(acc, c_ref, s1_ref, s2_ref):
    n = acc.shape[1]
    reps = n // c_ref.shape[1]
    tile = lambda r: jnp.concatenate([r[...]] * reps, axis=1)
    return acc * tile(c_ref) + pltpu.roll(acc, 16, 1) * tile(s1_ref) + pltpu.roll(acc, n - 16, 1) * tile(s2_ref)


def _mod_row_map(tm, t_lat, n_batch):
    return lambda i, *_: (jnp.minimum((i * tm) // t_lat, n_batch), 0, 0)


def _modulate_kernel(x_ref, mod_ref, h_ref, *, shift, scale):
    x = x_ref[...]
    h_ref[...] = (x * (1.0 + mod_ref[scale:scale + 1, :]) + mod_ref[shift:shift + 1, :]).astype(h_ref.dtype)


def _modulate(xa, mod, *, shift, scale, t_lat, n_batch, tm=256):
    m, d = xa.shape
    tm = _tile(math.gcd(m, t_lat), tm, SUBLANES)
    return pl.pallas_call(
        functools.partial(_modulate_kernel, shift=shift, scale=scale),
        grid=(m // tm,),
        in_specs=[pl.BlockSpec((tm, d), lambda i: (i, 0)),
                  pl.BlockSpec((None, ADA_MULT, d), _mod_row_map(tm, t_lat, n_batch))],
        out_specs=pl.BlockSpec((tm, d), lambda i: (i, 0)),
        out_shape=jax.ShapeDtypeStruct((m, d), BF16),
        compiler_params=_cparams(("parallel",)),
        name="modulate",
    )(xa, mod)


def _post_norm(x, y, gate_row, lnw_ref, lnb_ref, alpha):
    v = alpha * x + gate_row * y
    vc = v - jnp.mean(v, axis=-1, keepdims=True)
    var = jnp.mean(vc * vc, axis=-1, keepdims=True)
    return vc * lax.rsqrt(var + EPS) * lnw_ref[...] + lnb_ref[...]


def _ln_router_kernel(x_ref, y_ref, mod_ref, lnw_ref, lnb_ref, wr_ref, xo_ref, h_ref, lg_ref, *, alpha):
    xn = _post_norm(x_ref[...], y_ref[...], mod_ref[2:3, :], lnw_ref, lnb_ref, alpha)
    xo_ref[...] = xn
    h = xn * (1.0 + mod_ref[4:5, :]) + mod_ref[3:4, :]
    h_ref[...] = h
    lg_ref[...] = lax.dot_general(wr_ref[...], h.astype(BF16), (((1,), (1,)), ((), ())),
                                  preferred_element_type=F32)


def _ln_router(xa, y, mod, lnw, lnb, w_router_t, *, m, alpha, t_lat, n_batch, tm=256):
    d = xa.shape[1]
    e = w_router_t.shape[0]
    tm = _tile(math.gcd(m, t_lat), tm, LANES)
    row = lambda i: (i, 0)
    const = lambda i: (0, 0)
    return pl.pallas_call(
        functools.partial(_ln_router_kernel, alpha=alpha),
        grid=(m // tm,),
        in_specs=[pl.BlockSpec((tm, d), row), pl.BlockSpec((tm, d), row),
                  pl.BlockSpec((None, ADA_MULT, d), _mod_row_map(tm, t_lat, n_batch)),
                  pl.BlockSpec((1, d), const), pl.BlockSpec((1, d), const), pl.BlockSpec((e, d), const)],
        out_specs=[pl.BlockSpec((tm, d), row), pl.BlockSpec((tm, d), row), pl.BlockSpec((e, tm), lambda i: (0, i))],
        out_shape=[jax.ShapeDtypeStruct((m, d), F32), jax.ShapeDtypeStruct((m, d), F32),
                   jax.ShapeDtypeStruct((e, m), F32)],
        compiler_params=_cparams(("parallel",)),
        name="ln_router",
    )(xa, y, mod, lnw, lnb, w_router_t)


def _ln_combine_kernel(d0_ref, d1_ref, x_ref, y_ref, wt_ref, mod_ref, lnw_ref, lnb_ref, *rest, alpha, tm,
                       with_h):
    if with_h:
        modn_ref, xo_ref, h_ref, buf, sem = rest
    else:
        xo_ref, buf, sem = rest
    row0 = pl.program_id(0) * tm

    def row_copy(src_row, slot, r):
        return pltpu.make_async_copy(y_ref.at[pl.ds(src_row, 1)], buf.at[slot, pl.ds(r, 1)], sem)

    def issue(r, carry):
        row_copy(d0_ref[row0 + r], 0, r).start()
        row_copy(d1_ref[row0 + r], 1, r).start()
        return carry

    lax.fori_loop(0, tm, issue, 0)

    def wait(r, carry):
        row_copy(0, 0, 0).wait()
        row_copy(0, 1, 0).wait()
        return carry

    lax.fori_loop(0, tm, wait, 0)
    ffn = wt_ref[:, 0:1] * buf[0] + wt_ref[:, 1:2] * buf[1]
    xn = _post_norm(x_ref[...], ffn, mod_ref[5:6, :], lnw_ref, lnb_ref, alpha)
    xo_ref[...] = xn
    if with_h:
        h_ref[...] = (xn * (1.0 + modn_ref[1:2, :]) + modn_ref[0:1, :]).astype(h_ref.dtype)


def _ln_combine(x1, y_sorted, dest0, dest1, wt, mod, lnw, lnb, *, m, alpha, t_lat, n_batch, mod_next=None,
                tm=256):
    d = x1.shape[1]
    tm = _tile(math.gcd(m, t_lat), tm, SUBLANES)
    with_h = mod_next is not None
    row = lambda i, *_: (i, 0)
    const = lambda i, *_: (0, 0)
    modmap = _mod_row_map(tm, t_lat, n_batch)
    in_specs = [pl.BlockSpec((tm, d), row), pl.BlockSpec(memory_space=pl.ANY), pl.BlockSpec((tm, LANES), row),
                pl.BlockSpec((None, ADA_MULT, d), modmap), pl.BlockSpec((1, d), const), pl.BlockSpec((1, d), const)]
    args = [x1, y_sorted, wt, mod, lnw, lnb]
    out_specs = [pl.BlockSpec((tm, d), row)]
    out_shape = [jax.ShapeDtypeStruct((m, d), F32)]
    if with_h:
        in_specs.append(pl.BlockSpec((None, ADA_MULT, d), modmap))
        args.append(mod_next)
        out_specs.append(pl.BlockSpec((tm, d), row))
        out_shape.append(jax.ShapeDtypeStruct((m, d), BF16))
    return pl.pallas_call(
        functools.partial(_ln_combine_kernel, alpha=alpha, tm=tm, with_h=with_h),
        grid_spec=pltpu.PrefetchScalarGridSpec(
            num_scalar_prefetch=2, grid=(m // tm,), in_specs=in_specs, out_specs=out_specs,
            scratch_shapes=[pltpu.VMEM((TOP_K, tm, d), F32), pltpu.SemaphoreType.DMA(())]),
        out_shape=out_shape,
        compiler_params=_cparams(("arbitrary",)),
        name="ln_combine",
    )(dest0, dest1, *args)


def _seq_edges(i, tm, n_lat, t_lat, t_ctx):
    row0 = i * tm
    in_lat = row0 < n_lat
    start = jnp.where(in_lat, row0 % t_lat == 0, (row0 - n_lat) % t_ctx == 0)
    end = jnp.where(in_lat, (row0 + tm) % t_lat == 0, (row0 + tm - n_lat) % t_ctx == 0)
    return start, end


def _conv3(u, prev_row, next_row, w_ref):
    tm = u.shape[0]
    rows = lax.broadcasted_iota(jnp.int32, u.shape, 0)
    um1 = jnp.where(rows == 0, prev_row, pltpu.roll(u, 1, 0))
    up1 = jnp.where(rows == tm - 1, next_row, pltpu.roll(u, tm - 1, 0))
    return w_ref[0:1, :] * um1 + w_ref[1:2, :] * u + w_ref[2:3, :] * up1


def _halo_specs(tm, tc, col_of_j, n_rows):
    last = n_rows // HALO - 1
    prev = pl.BlockSpec((HALO, tc), lambda i, j: (jnp.maximum(i * (tm // HALO) - 1, 0), col_of_j(j)))
    nxt = pl.BlockSpec((HALO, tc), lambda i, j: (jnp.minimum((i + 1) * (tm // HALO), last), col_of_j(j)))
    return prev, nxt


def _sconv_kernel(b_ref, c_ref, x_ref, cp_ref, xp_ref, cn_ref, xn_ref, w_ref, o_ref, *, tm, n_lat, t_lat,
                  t_ctx):
    start, end = _seq_edges(pl.program_id(0), tm, n_lat, t_lat, t_ctx)
    u = c_ref[...] * x_ref[...]
    prev = jnp.where(start, 0.0, cp_ref[HALO - 1:HALO, :] * xp_ref[HALO - 1:HALO, :])
    nxt = jnp.where(end, 0.0, cn_ref[0:1, :] * xn_ref[0:1, :])
    o_ref[...] = (b_ref[...] * _conv3(u, prev, nxt, w_ref)).astype(o_ref.dtype)


def _short_conv(p_sc, w, *, m, n_lat, t_lat, t_ctx, tm=256, tc=1024):
    n_rows = p_sc.shape[0]
    width = SC_WIDTH
    tc = _tile(width, tc, LANES)
    tm = _tile(math.gcd(t_lat, t_ctx), tm, HALO)
    nb = width // tc
    blk = lambda g: pl.BlockSpec((tm, tc), lambda i, j: (i, g * nb + j))
    cp, cn = _halo_specs(tm, tc, lambda j: nb + j, n_rows)
    xp, xn = _halo_specs(tm, tc, lambda j: 2 * nb + j, n_rows)
    return pl.pallas_call(
        functools.partial(_sconv_kernel, tm=tm, n_lat=n_lat, t_lat=t_lat, t_ctx=t_ctx),
        grid=(m // tm, nb),
        in_specs=[blk(0), blk(1), blk(2), cp, xp, cn, xn, pl.BlockSpec((3, tc), lambda i, j: (0, j))],
        out_specs=pl.BlockSpec((tm, tc), lambda i, j: (i, j)),
        out_shape=jax.ShapeDtypeStruct((m, width), BF16),
        compiler_params=_cparams(("parallel", "parallel")),
        name="short_conv",
    )(p_sc, p_sc, p_sc, p_sc, p_sc, p_sc, p_sc, w)


def _small_kernel(p_ref, qn_ref, kvn_ref, ck_ref, s1_ref, s2_ref, nexp_ref, dtb_ref, cq_ref, kv_ref, gb_ref):
    cq = p_ref[:, :MLA_Q_RANK]
    cq_ref[...] = (cq * lax.rsqrt(jnp.mean(cq * cq, axis=-1, keepdims=True) + EPS) * qn_ref[...]).astype(
        cq_ref.dtype)
    ckv = p_ref[:, MLA_Q_RANK:MLA_Q_RANK + MLA_KV_RANK]
    kv_ref[:, :MLA_KV_RANK] = (ckv * lax.rsqrt(jnp.mean(ckv * ckv, axis=-1, keepdims=True) + EPS)
                               * kvn_ref[...]).astype(kv_ref.dtype)
    x = p_ref[:, MLA_Q_RANK + MLA_KV_RANK:]
    rot = x * ck_ref[...] + pltpu.roll(x, 16, 1) * s1_ref[...] + pltpu.roll(x, LANES - 16, 1) * s2_ref[...]
    kv_ref[:, MLA_KV_RANK:] = rot.astype(kv_ref.dtype)
    z = x + dtb_ref[...]
    g = nexp_ref[...] * (jnp.maximum(z, 0.0) + jnp.log1p(jnp.exp(-jnp.abs(z))))
    tm = x.shape[0]
    rows = lax.broadcasted_iota(jnp.int32, x.shape, 0) % GDN_CHUNK
    lanes = lax.broadcasted_iota(jnp.int32, x.shape, 1)
    fwd = g
    rev = g
    s = 1
    while s < GDN_CHUNK:
        fwd = fwd + jnp.where(rows >= s, pltpu.roll(fwd, s, 0), 0.0)
        rev = rev + jnp.where(rows < GDN_CHUNK - s, pltpu.roll(rev, tm - s, 0), 0.0)
        s *= 2
    beta = jax.nn.sigmoid(x)
    gb_ref[...] = jnp.where(lanes < 64, 0.0, jnp.where(lanes < 80, fwd, jnp.where(lanes < 96, rev, beta)))


def _small_prologue(p_small, q_norm, kv_norm, ck, s1, s2, nexp, dtb, *, tm=256):
    m = p_small.shape[0]
    tm = _tile(m, tm, GDN_CHUNK)
    row = lambda i: (i, 0)
    vec = lambda w: pl.BlockSpec((1, w), lambda i: (0, 0))
    tab = pl.BlockSpec((tm, LANES), row)
    return pl.pallas_call(
        _small_kernel,
        grid=(m // tm,),
        in_specs=[pl.BlockSpec((tm, SMALL_W), row), vec(MLA_Q_RANK), vec(MLA_KV_RANK), tab, tab, tab,
                  vec(LANES), vec(LANES)],
        out_specs=[pl.BlockSpec((tm, MLA_Q_RANK), row), pl.BlockSpec((tm, KV_LHS_W), row),
                   pl.BlockSpec((tm, LANES), row)],
        out_shape=[jax.ShapeDtypeStruct((m, MLA_Q_RANK), BF16), jax.ShapeDtypeStruct((m, KV_LHS_W), BF16),
                   jax.ShapeDtypeStruct((m, LANES), F32)],
        compiler_params=_cparams(("parallel",)),
        name="small_prologue",
    )(p_small, q_norm, kv_norm, ck, s1, s2, nexp, dtb)


def _attn_kernel(*refs, with_lat):
    if with_lat:
        q_ref, kc_ref, vc_ref, kl_ref, vl_ref, o_ref = refs
    else:
        q_ref, kc_ref, vc_ref, o_ref = refs
    nt = (((1,), (1,)), ((), ()))
    heads = range(ATTN_HEADS)
    ks = lambda hh: slice(hh * MLA_PAD, (hh + 1) * MLA_PAD)
    vs = lambda hh: slice(hh * MLA_V, (hh + 1) * MLA_V)
    q = [q_ref[:, ks(hh)] for hh in heads]
    sc = [lax.dot_general(q[hh], kc_ref[:, ks(hh)], nt, preferred_element_type=F32) for hh in heads]
    mx = [jnp.max(sc[hh], axis=-1, keepdims=True) for hh in heads]
    if with_lat:
        sl = [lax.dot_general(q[hh], kl_ref[:, ks(hh)], nt, preferred_element_type=F32) for hh in heads]
        mx = [jnp.maximum(mx[hh], jnp.max(sl[hh], axis=-1, keepdims=True)) for hh in heads]
    pc = [jnp.exp(sc[hh] - mx[hh]) for hh in heads]
    den = [jnp.sum(pc[hh], axis=-1, keepdims=True) for hh in heads]
    acc = [jnp.dot(pc[hh].astype(BF16), vc_ref[:, vs(hh)], preferred_element_type=F32) for hh in heads]
    if with_lat:
        pl_ = [jnp.exp(sl[hh] - mx[hh]) for hh in heads]
        den = [den[hh] + jnp.sum(pl_[hh], axis=-1, keepdims=True) for hh in heads]
        acc = [acc[hh] + jnp.dot(pl_[hh].astype(BF16), vl_ref[:, vs(hh)], preferred_element_type=F32)
               for hh in heads]
    for hh in heads:
        o_ref[:, vs(hh)] = (acc[hh] / den[hh]).astype(o_ref.dtype)


def _attention(q, kv, *, n_batch, t_lat, t_ctx, latent_queries, tq=256):
    n_lat = n_batch * t_lat
    kw = ATTN_HEADS * MLA_PAD
    vw = ATTN_HEADS * MLA_V
    vcol0 = MLA_HEADS * MLA_PAD // vw
    if latent_queries:
        tq = _tile(t_lat, tq, SUBLANES)
        nq = t_lat // tq
        m_out = n_lat
        qrow = lambda b, hh, qi: (b * nq + qi, hh)
        orow = qrow
    else:
        tq = t_ctx
        nq = 1
        m_out = n_batch * t_ctx
        qrow = lambda b, hh, qi: (n_lat // t_ctx + b, hh)
        orow = lambda b, hh, qi: (b, hh)
    ctx_blk = n_lat // t_ctx
    in_specs = [pl.BlockSpec((tq, kw), qrow),
                pl.BlockSpec((t_ctx, kw), lambda b, hh, qi: (ctx_blk + b, hh)),
                pl.BlockSpec((t_ctx, vw), lambda b, hh, qi: (ctx_blk + b, vcol0 + hh))]
    args = [q, kv, kv]
    if latent_queries:
        in_specs += [pl.BlockSpec((t_lat, kw), lambda b, hh, qi: (b, hh)),
                     pl.BlockSpec((t_lat, vw), lambda b, hh, qi: (b, vcol0 + hh))]
        args += [kv, kv]
    return pl.pallas_call(
        functools.partial(_attn_kernel, with_lat=latent_queries),
        grid=(n_batch, MLA_HEADS // ATTN_HEADS, nq),
        in_specs=in_specs,
        out_specs=pl.BlockSpec((tq, vw), orow),
        out_shape=jax.ShapeDtypeStruct((m_out, MLA_HEADS * MLA_V), BF16),
        compiler_params=_cparams(("parallel", "parallel", "arbitrary")),
        name="attention_lat" if latent_queries else "attention_ctx",
    )(*args)


def _gdn_feat_kernel(x_ref, xp_ref, xn_ref, w_ref, o_ref, *, tm, tc, n_lat, t_lat, t_ctx):
    start, end = _seq_edges(pl.program_id(0), tm, n_lat, t_lat, t_ctx)
    j = pl.program_id(1)
    prev = jnp.where(start, 0.0, xp_ref[HALO - 1:HALO, :])
    nxt = jnp.where(end, 0.0, xn_ref[0:1, :])
    f = _silu(_conv3(x_ref[...], prev, nxt, w_ref))
    is_q = j < GDN_QK // tc
    is_v = j >= 2 * GDN_QK // tc
    post = jnp.where(is_q, GDN_DK ** -0.5, 1.0)
    for hh in range(tc // GDN_DK):
        fh = f[:, hh * GDN_DK:(hh + 1) * GDN_DK]
        inv = lax.rsqrt(jnp.sum(fh * fh, axis=-1, keepdims=True) + EPS) * post
        o_ref[:, hh * GDN_DK:(hh + 1) * GDN_DK] = fh * jnp.where(is_v, 1.0, inv)


def _gdn_features(p_gdn, w, *, n_lat, t_lat, t_ctx, tm=256, tc=1024):
    n_rows = p_gdn.shape[0]
    width = 2 * GDN_QK + GDN_VW
    tc = _tile(GDN_QK, tc, GDN_DK)
    tm = _tile(math.gcd(t_lat, t_ctx), tm, HALO)
    xp, xn = _halo_specs(tm, tc, lambda j: j, n_rows)
    return pl.pallas_call(
        functools.partial(_gdn_feat_kernel, tm=tm, tc=tc, n_lat=n_lat, t_lat=t_lat, t_ctx=t_ctx),
        grid=(n_rows // tm, width // tc),
        in_specs=[pl.BlockSpec((tm, tc), lambda i, j: (i, j)), xp, xn,
                  pl.BlockSpec((3, tc), lambda i, j: (0, j))],
        out_specs=pl.BlockSpec((tm, tc), lambda i, j: (i, j)),
        out_shape=jax.ShapeDtypeStruct((n_rows, width), F32),
        compiler_params=_cparams(("parallel", "parallel")),
        name="gdn_features",
    )(p_gdn, p_gdn, p_gdn, w)


def _block_diag(xp, blk, n):
    return jnp.concatenate([jnp.where(blk == p, xp, jnp.zeros_like(xp)) for p in range(n)], axis=0)


def _gdn_scan_kernel(fq0, fk0, fv0, gb0, gr0, fq1, fk1, fv1, gb1, gr1, o0_ref, o1_ref, state_ref):
    c = GDN_CHUNK
    pw = QUAD * c
    nt = (((1,), (1,)), ((), ()))
    tn = (((0,), (0,)), ((), ()))

    @pl.when(pl.program_id(1) == 0)
    def _():
        state_ref[...] = jnp.zeros_like(state_ref)

    ri = lax.broadcasted_iota(jnp.int32, (c, pw), 0)
    ci = lax.broadcasted_iota(jnp.int32, (c, pw), 1)
    cj = ci % c
    blk = ci // c
    blk_k = lax.broadcasted_iota(jnp.int32, (c, QUAD * GDN_DK), 1) // GDN_DK
    eye_p = jnp.where(ri == cj, 1.0, 0.0)
    lanes = lax.broadcasted_iota(jnp.int32, (c, LANES), 1)
    dirs = ((fq0, fk0, fv0, gb0, gr0), (fq1, fk1, fv1, gb1, gr1))
    nq = GDN_HEADS // QUAD

    quads = []
    for d, (fq, fk, fv, gb_ref, gr_ref) in enumerate(dirs):
        incl = (ri <= cj) if d == 1 else (ri >= cj)
        strict = (ri < cj) if d == 1 else (ri > cj)
        gbv = gb_ref[...]
        for g in range(nq):
            cols = slice(g * QUAD * GDN_DK, (g + 1) * QUAD * GDN_DK)
            k4, q4, v4 = fk[:, cols], fq[:, cols], fv[:, cols]
            grow = gr_ref[d * nq + g:d * nq + g + 1, :]
            heads, kb_l, gcol_p = [], [], None
            for p in range(QUAD):
                h = g * QUAD + p
                hs = slice(p * GDN_DK, (p + 1) * GDN_DK)
                gcol = jnp.sum(jnp.where(lanes == 64 + d * GDN_HEADS + h, gbv, 0.0), axis=1, keepdims=True)
                bcol = jnp.sum(jnp.where(lanes == 96 + d * GDN_HEADS + h, gbv, 0.0), axis=1, keepdims=True)
                gend = grow[:, p * c:p * c + 1] if d == 1 else grow[:, (p + 1) * c - 1:(p + 1) * c]
                egc = jnp.exp(gcol)
                kp = k4[:, hs]
                kb = kp * bcol
                kb_l.append(kb)
                heads.append(dict(
                    h=h,
                    rhs=jnp.concatenate([v4[:, hs] * bcol, kb * egc], axis=1).astype(BF16),
                    qe=q4[:, hs] * egc,
                    kt=(kp * jnp.exp(gend - gcol)).astype(BF16),
                    cd=jnp.exp(gend),
                ))
                gcol_p = gcol if gcol_p is None else jnp.where(blk == p, gcol, gcol_p)
            decay = jnp.exp(jnp.where(incl, gcol_p - grow, -jnp.inf))
            kbd = _block_diag(k4.astype(BF16), blk_k, QUAD)
            lhs = jnp.concatenate([jnp.concatenate(kb_l, axis=1), q4], axis=0).astype(BF16)
            quads.append(dict(d=d, heads=heads, decay=decay, strict=strict, kbd=kbd, lhs=lhs))

    for qd in quads:
        sc = lax.dot_general(qd["lhs"], qd["kbd"], nt, preferred_element_type=F32)
        qd["x"] = -jnp.where(qd["strict"], sc[:c] * qd["decay"], 0.0)
        qd["intra"] = (sc[c:] * qd["decay"]).astype(BF16)
        qd["tinv"] = eye_p + qd["x"]

    def split(v):
        hi = v.astype(BF16)
        return hi, (v - hi.astype(F32)).astype(BF16)

    def times_bd(v, bd_hi, bd_lo):
        hi, lo = split(v)
        r = jnp.dot(jnp.concatenate([hi, lo], axis=0), bd_hi, preferred_element_type=F32)
        return r[:c] + r[c:] + jnp.dot(hi, bd_lo, preferred_element_type=F32)

    def bd_split(v):
        hi, lo = split(v)
        return _block_diag(hi, blk, QUAD), _block_diag(lo, blk, QUAD)

    for qd in quads:
        qd["bd"] = bd_split(qd["x"])
    step = 2
    while step < c:
        for qd in quads:
            qd["x"] = times_bd(qd["x"], *qd["bd"])
        for qd in quads:
            qd["bd"] = bd_split(qd["x"])
        for qd in quads:
            qd["tinv"] = qd["tinv"] + times_bd(qd["tinv"], *qd["bd"])
        step *= 2

    for qd in quads:
        t16 = qd["tinv"].astype(BF16)
        for p, hd in enumerate(qd["heads"]):
            hd["uw"] = jnp.dot(t16[:, p * c:(p + 1) * c], hd["rhs"], preferred_element_type=F32)

    for qd in quads:
        for hd in qd["heads"]:
            hd["st"] = state_ref[qd["d"], hd["h"]]
            lhs = jnp.concatenate([hd["uw"][:, GDN_DV:], hd["qe"]], axis=0).astype(BF16)
            hd["ws"] = jnp.dot(lhs, hd["st"].astype(BF16), preferred_element_type=F32)
    for qd in quads:
        for hd in qd["heads"]:
            hd["vn"] = (hd["uw"][:, :GDN_DV] - hd["ws"][:c]).astype(BF16)
    outs = ([], [])
    for qd in quads:
        for p, hd in enumerate(qd["heads"]):
            outs[qd["d"]].append(hd["ws"][c:] + jnp.dot(qd["intra"][:, p * c:(p + 1) * c], hd["vn"],
                                                         preferred_element_type=F32))
            state_ref[qd["d"], hd["h"]] = hd["st"] * hd["cd"] + lax.dot_general(
                hd["kt"], hd["vn"], tn, preferred_element_type=F32)
    o0_ref[...] = jnp.concatenate(outs[0], axis=1)
    o1_ref[...] = jnp.concatenate(outs[1], axis=1)


def _gdn_scan(feat, gb, grows, *, n_batch, t_lat, t_ctx):
    n_rows = feat.shape[0]
    c = GDN_CHUNK
    n_lat = n_batch * t_lat
    nc_ctx, nc_lat = t_ctx // c, t_lat // c

    def chunk_row(d):
        def f(b, s):
            if d == 0:
                return jnp.where(s < nc_ctx, n_lat // c + b * nc_ctx + s, b * nc_lat + (s - nc_ctx))
            return jnp.where(s < nc_ctx, n_lat // c + b * nc_ctx + (nc_ctx - 1 - s),
                             b * nc_lat + (nc_lat - 1 - (s - nc_ctx)))
        return f

    in_specs, args = [], []
    for d in range(N_DIR):
        cr = chunk_row(d)
        for col0 in range(3):
            in_specs.append(pl.BlockSpec((c, GDN_QK), lambda b, s, cr=cr, col0=col0: (cr(b, s), col0)))
            args.append(feat)
        in_specs.append(pl.BlockSpec((c, LANES), lambda b, s, cr=cr: (cr(b, s), 0)))
        args.append(gb)
        in_specs.append(pl.BlockSpec((None,) + grows.shape[1:], lambda b, s, cr=cr: (cr(b, s), 0, 0)))
        args.append(grows)
    out_specs = [pl.BlockSpec((c, GDN_VW), lambda b, s, cr=chunk_row(d): (cr(b, s), 0)) for d in range(N_DIR)]
    return pl.pallas_call(
        _gdn_scan_kernel,
        grid=(n_batch, nc_ctx + nc_lat),
        in_specs=in_specs, out_specs=out_specs,
        out_shape=[jax.ShapeDtypeStruct((n_rows, GDN_VW), F32)] * N_DIR,
        scratch_shapes=[pltpu.VMEM((N_DIR, GDN_HEADS, GDN_DK, GDN_DV), F32)],
        compiler_params=_cparams(("parallel", "arbitrary")),
        name="gdn_scan",
    )(*args)


def _pack_decay_rows(gb):
    n_chunks = gb.shape[0] // GDN_CHUNK
    g = gb[:, 64:96].reshape(n_chunks, GDN_CHUNK, N_DIR, GDN_HEADS // QUAD, QUAD)
    return jnp.transpose(g, (0, 2, 3, 4, 1)).reshape(n_chunks, N_DIR * GDN_HEADS // QUAD, QUAD * GDN_CHUNK)


def _gdn_out_kernel(o0_ref, o1_ref, z_ref, w_ref, y_ref):
    o = o0_ref[...] + o1_ref[...]
    z = z_ref[...]
    for hh in range(o.shape[1] // GDN_DV):
        sl = slice(hh * GDN_DV, (hh + 1) * GDN_DV)
        oh = o[:, sl]
        yh = oh * lax.rsqrt(jnp.mean(oh * oh, axis=-1, keepdims=True) + EPS) * w_ref[...]
        y_ref[:, sl] = (yh * _silu(z[:, sl])).astype(y_ref.dtype)


def _gdn_output(o0, o1, p_gdn, w, *, m, tm=256, tc=1024):
    tc = _tile(GDN_VW, tc, GDN_DV)
    tm = _tile(m, tm, 16)
    z0 = (2 * GDN_QK + GDN_VW) // tc
    blk = pl.BlockSpec((tm, tc), lambda i, j: (i, j))
    return pl.pallas_call(
        _gdn_out_kernel,
        grid=(m // tm, GDN_VW // tc),
        in_specs=[blk, blk, pl.BlockSpec((tm, tc), lambda i, j: (i, z0 + j)),
                  pl.BlockSpec((1, GDN_DV), lambda i, j: (0, 0))],
        out_specs=blk,
        out_shape=jax.ShapeDtypeStruct((m, GDN_VW), BF16),
        compiler_params=_cparams(("parallel", "parallel")),
        name="gdn_output",
    )(o0, o1, p_gdn, w)


def _merge_kernel(a0, a1, a2, w_ref, g0, g1, g2, o_ref):
    acc = g0[...].astype(F32) * jnp.dot(a0[...], w_ref[0], preferred_element_type=F32)
    acc += g1[...].astype(F32) * jnp.dot(a1[...], w_ref[1], preferred_element_type=F32)
    acc += g2[...].astype(F32) * jnp.dot(a2[...], w_ref[2], preferred_element_type=F32)
    o_ref[...] = acc.astype(o_ref.dtype)


def _merge(attn, conv, gdn, w_branch, layer, gates, *, m, tm=512, tn=512):
    d = w_branch.shape[-1]
    tm = _tile(m, tm, 16)
    tn = _tile(d, tn, LANES)
    nb = d // tn
    lhs = pl.BlockSpec((tm, BRANCH_W), lambda j, i: (i, 0))
    gate = lambda n: pl.BlockSpec((tm, tn), lambda j, i: (i, n * nb + j))
    return pl.pallas_call(
        _merge_kernel,
        grid=(nb, m // tm),
        in_specs=[lhs, lhs, lhs, pl.BlockSpec((None, N_BRANCH, BRANCH_W, tn), lambda j, i: (layer, 0, 0, j)),
                  gate(0), gate(1), gate(2)],
        out_specs=pl.BlockSpec((tm, tn), lambda j, i: (i, j)),
        out_shape=jax.ShapeDtypeStruct((m, d), BF16),
        compiler_params=_cparams(("parallel", "parallel")),
        name="merge",
    )(attn, conv, gdn, w_branch, gates, gates, gates)


def _route_kernel(lg_ref, bias_ref, idx_ref, rank_ref, wt_ref, cnt_ref, run_ref):
    @pl.when(pl.program_id(0) == 0)
    def _():
        run_ref[...] = jnp.zeros_like(run_ref)

    aff = jax.nn.sigmoid(lg_ref[...])
    sel = aff + bias_ref[...]
    e, tn = aff.shape
    eidx = lax.broadcasted_iota(jnp.int32, (e, tn), 0)
    best = None
    for g in range(N_GROUPS):
        r = [sel[g * EXPERTS_PER_GROUP + t:g * EXPERTS_PER_GROUP + t + 1, :] for t in range(EXPERTS_PER_GROUP)]
        top2 = None
        for a in range(EXPERTS_PER_GROUP):
            for b in range(a + 1, EXPERTS_PER_GROUP):
                s = r[a] + r[b]
                top2 = s if top2 is None else jnp.maximum(top2, s)
        if best is None:
            best, gidx = top2, jnp.zeros_like(top2, dtype=jnp.int32)
        else:
            better = top2 > best
            best = jnp.where(better, top2, best)
            gidx = jnp.where(better, g, gidx)
    masked = jnp.where(eidx // EXPERTS_PER_GROUP == gidx, sel, -jnp.inf)
    m1 = jnp.max(masked, axis=0, keepdims=True)
    i1 = jnp.min(jnp.where(masked == m1, eidx, e), axis=0, keepdims=True)
    masked2 = jnp.where(eidx == i1, -jnp.inf, masked)
    m2 = jnp.max(masked2, axis=0, keepdims=True)
    i2 = jnp.min(jnp.where(masked2 == m2, eidx, e), axis=0, keepdims=True)
    hot1 = eidx == i1
    hot2 = eidx == i2
    w1 = jnp.sum(jnp.where(hot1, aff, 0.0), axis=0, keepdims=True)
    w2 = jnp.sum(jnp.where(hot2, aff, 0.0), axis=0, keepdims=True)
    den = w1 + w2
    hot = jnp.where(hot1 | hot2, 1.0, 0.0)
    ra = lax.broadcasted_iota(jnp.int32, (tn, tn), 0)
    rb = lax.broadcasted_iota(jnp.int32, (tn, tn), 1)
    upper = jnp.where(ra <= rb, 1.0, 0.0).astype(BF16)
    incl = jnp.dot(hot.astype(BF16), upper, preferred_element_type=F32)
    pos = run_ref[...] + incl - hot
    rank1 = jnp.sum(jnp.where(hot1, pos, 0.0), axis=0, keepdims=True)
    rank2 = jnp.sum(jnp.where(hot2, pos, 0.0), axis=0, keepdims=True)
    run_ref[...] = run_ref[...] + incl[:, tn - 1:tn]
    idx_ref[...] = jnp.concatenate([i1, i2], axis=0)
    rank_ref[...] = jnp.concatenate([rank1, rank2], axis=0).astype(jnp.int32)
    wrow = jnp.concatenate([w1 / den, w2 / den, jnp.zeros((LANES - TOP_K, tn), F32)], axis=0)
    wt_ref[...] = wrow.T
    cnt_ref[...] = run_ref[...].astype(jnp.int32)


def _route(logits, bias_col, *, tn=512):
    e, m = logits.shape
    tn = _tile(m, tn, LANES)
    row2 = pl.BlockSpec((TOP_K, tn), lambda i: (0, i))
    return pl.pallas_call(
        _route_kernel,
        grid=(m // tn,),
        in_specs=[pl.BlockSpec((e, tn), lambda i: (0, i)), pl.BlockSpec((e, 1), lambda i: (0, 0))],
        out_specs=[row2, row2, pl.BlockSpec((tn, LANES), lambda i: (i, 0)), pl.BlockSpec((e, 1), lambda i: (0, 0))],
        out_shape=[jax.ShapeDtypeStruct((TOP_K, m), jnp.int32), jax.ShapeDtypeStruct((TOP_K, m), jnp.int32),
                   jax.ShapeDtypeStruct((m, LANES), F32), jax.ShapeDtypeStruct((e, 1), jnp.int32)],
        scratch_shapes=[pltpu.VMEM((e, 1), F32)],
        compiler_params=_cparams(("arbitrary",)),
        name="route",
    )(logits, bias_col)


def _moe_plan(counts, n_tokens):
    cnt = counts.reshape(-1)
    gsz = ((cnt + MOE_TM - 1) // MOE_TM) * MOE_TM
    gend = jnp.cumsum(gsz)
    gstart = gend - gsz
    nt = (TOP_K * n_tokens + MOE_TM - 1) // MOE_TM + N_EXPERTS
    used = (gend[-1] // MOE_TM).astype(jnp.int32)
    tile0 = jnp.arange(nt, dtype=jnp.int32) * MOE_TM
    te = jnp.minimum(jnp.sum(gend[None, :] <= tile0[:, None], axis=1), N_EXPERTS - 1).astype(jnp.int32)
    te = jnp.where(jnp.arange(nt) < used, te, te[jnp.maximum(used - 1, 0)])
    pad1 = gend.at[-1].set(nt * MOE_TM)
    return dict(gstart=gstart.astype(jnp.int32).reshape(-1, 1), pad0=(gstart + cnt).astype(jnp.int32),
                pad1=pad1.astype(jnp.int32), tile_expert=te, used=used.reshape(1), n_rows=nt * MOE_TM)


def _dest_kernel(idx_ref, rank_ref, start_ref, dest_ref):
    idx = idx_ref[...]
    out = rank_ref[...]
    for ex in range(N_EXPERTS):
        out = out + jnp.where(idx == ex, start_ref[ex:ex + 1, :], 0)
    dest_ref[...] = out


def _dest(idx, rank, gstart_col, *, tn=512):
    k, m = idx.shape
    tn = _tile(m, tn, LANES)
    row = pl.BlockSpec((k, tn), lambda i: (0, i))
    return pl.pallas_call(
        _dest_kernel,
        grid=(m // tn,),
        in_specs=[row, row, pl.BlockSpec((N_EXPERTS, 1), lambda i: (0, 0))],
        out_specs=row,
        out_shape=jax.ShapeDtypeStruct((k, m), jnp.int32),
        compiler_params=_cparams(("parallel",)),
        name="moe_dest",
    )(idx, rank, gstart_col)


def _dispatch_kernel(d0_ref, d1_ref, pad0_ref, pad1_ref, h_ref, xs_ref, sem, *, tm):
    row0 = pl.program_id(0) * tm

    def row_copy(src_row, dst_row):
        return pltpu.make_async_copy(h_ref.at[pl.ds(src_row, 1)], xs_ref.at[pl.ds(dst_row, 1)], sem)

    def wait_rows(n):
        def body(r, carry):
            row_copy(0, 0).wait()
            return carry
        lax.fori_loop(0, n, body, 0)

    def scatter_token(r, carry):
        row_copy(r, d0_ref[row0 + r]).start()
        row_copy(r, d1_ref[row0 + r]).start()
        return carry

    lax.fori_loop(0, tm, scatter_token, 0)
    wait_rows(TOP_K * tm)

    @pl.when(pl.program_id(0) == 0)
    def _():
        for ex in range(N_EXPERTS):
            lo, hi = pad0_ref[ex], pad1_ref[ex]

            def fill(r, carry, lo=lo):
                row_copy(0, r).start()

                @pl.when(r - lo >= DMA_WINDOW)
                def _():
                    wait_rows(1)
                return carry

            lax.fori_loop(lo, hi, fill, 0)
            wait_rows(jnp.minimum(hi - lo, DMA_WINDOW))


def _dispatch(h, dest0, dest1, pad0, pad1, *, n_rows, tm=256):
    m, d = h.shape
    tm = _tile(m, tm, SUBLANES)
    return pl.pallas_call(
        functools.partial(_dispatch_kernel, tm=tm),
        grid_spec=pltpu.PrefetchScalarGridSpec(
            num_scalar_prefetch=4, grid=(m // tm,),
            in_specs=[pl.BlockSpec((tm, d), lambda i, *_: (i, 0))],
            out_specs=pl.BlockSpec(memory_space=pl.ANY),
            scratch_shapes=[pltpu.SemaphoreType.DMA(())]),
        out_shape=jax.ShapeDtypeStruct((n_rows, d), h.dtype),
        compiler_params=_cparams(("arbitrary",)),
        name="moe_dispatch",
    )(dest0, dest1, pad0, pad1, h)


def _new_expert_weights(te_ref):
    i = pl.program_id(1)
    return jnp.logical_or(i == 0, te_ref[i] != te_ref[jnp.maximum(i - 1, 0)])


def _expert_up_kernel(te_ref, used_ref, x_ref, wg_ref, wu_ref, o_ref, wg16, wu16):
    active = pl.program_id(1) < used_ref[0]

    @pl.when(_new_expert_weights(te_ref))
    def _():
        wg16[...] = wg_ref[...].astype(BF16)
        wu16[...] = wu_ref[...].astype(BF16)

    @pl.when(active)
    def _():
        x = x_ref[...].astype(BF16)
        g = jnp.dot(x, wg16[...], preferred_element_type=F32)
        u = jnp.dot(x, wu16[...], preferred_element_type=F32)
        o_ref[...] = (_silu(g) * u).astype(o_ref.dtype)

    @pl.when(jnp.logical_not(active))
    def _():
        o_ref[...] = jnp.zeros_like(o_ref)


def _expert_up(xs, wg, wu, layer, tile_expert, used, *, tn=512):
    r, d = xs.shape
    f = wg.shape[-1]
    tn = _tile(f, tn, LANES)
    wspec = pl.BlockSpec((None, None, d, tn), lambda j, i, te, us: (layer, te[i], 0, j))
    return pl.pallas_call(
        _expert_up_kernel,
        grid_spec=pltpu.PrefetchScalarGridSpec(
            num_scalar_prefetch=2, grid=(f // tn, r // MOE_TM),
            in_specs=[pl.BlockSpec((MOE_TM, d), lambda j, i, te, us: (i, 0)), wspec, wspec],
            out_specs=pl.BlockSpec((MOE_TM, tn), lambda j, i, te, us: (i, j)),
            scratch_shapes=[pltpu.VMEM((d, tn), BF16), pltpu.VMEM((d, tn), BF16)]),
        out_shape=jax.ShapeDtypeStruct((r, f), BF16),
        compiler_params=_cparams(("arbitrary", "arbitrary")),
        name="moe_up",
    )(tile_expert, used, xs, wg, wu)


def _expert_down_kernel(te_ref, used_ref, a_ref, wd_ref, o_ref, wd16):
    active = pl.program_id(1) < used_ref[0]

    @pl.when(_new_expert_weights(te_ref))
    def _():
        wd16[...] = wd_ref[...].astype(BF16)

    @pl.when(active)
    def _():
        o_ref[...] = jnp.dot(a_ref[...], wd16[...], preferred_element_type=F32)

    @pl.when(jnp.logical_not(active))
    def _():
        o_ref[...] = jnp.zeros_like(o_ref)


def _expert_down(act, wd, layer, tile_expert, used, *, tn=2048):
    r, f = act.shape
    d = wd.shape[-1]
    tn = _tile(d, tn, LANES)
    return pl.pallas_call(
        _expert_down_kernel,
        grid_spec=pltpu.PrefetchScalarGridSpec(
            num_scalar_prefetch=2, grid=(d // tn, r // MOE_TM),
            in_specs=[pl.BlockSpec((MOE_TM, f), lambda j, i, te, us: (i, 0)),
                      pl.BlockSpec((None, None, f, tn), lambda j, i, te, us: (layer, te[i], 0, j))],
            out_specs=pl.BlockSpec((MOE_TM, tn), lambda j, i, te, us: (i, j)),
            scratch_shapes=[pltpu.VMEM((f, tn), BF16)]),
        out_shape=jax.ShapeDtypeStruct((r, d), F32),
        compiler_params=_cparams(("arbitrary", "arbitrary")),
        name="moe_down",
    )(tile_expert, used, act, wd)


def _rope_tables(n_batch, t_lat, t_ctx, width, lane0, scale):
    nf = MLA_ROPE // 4
    pos = jnp.arange(t_lat)
    inv = ROPE_THETA ** (-jnp.arange(nf, dtype=F32) / nf)
    ang_r = (pos // GRID_W).astype(F32)[:, None] * inv
    ang_c = (pos % GRID_W).astype(F32)[:, None] * inv
    z = jnp.zeros((t_lat, nf), F32)
    cos = jnp.concatenate([jnp.cos(ang_r)] * 2 + [jnp.cos(ang_c)] * 2, axis=1)
    s1 = jnp.concatenate([z, jnp.sin(ang_r), z, jnp.sin(ang_c)], axis=1)
    s2 = jnp.concatenate([-jnp.sin(ang_r), z, -jnp.sin(ang_c), z], axis=1)

    def place(rot, ctx_rot, fill):
        def rows(r, n):
            left = jnp.full((n, lane0), fill, F32)
            right = jnp.zeros((n, width - lane0 - MLA_ROPE), F32)
            return jnp.concatenate([left, r, right], axis=1)
        lat = rows(rot, t_lat)
        ctx = rows(jnp.broadcast_to(ctx_rot, (t_ctx, MLA_ROPE)), t_ctx)
        return jnp.concatenate([lat] * n_batch + [ctx] * n_batch, axis=0) * scale

    return (place(cos, jnp.ones((MLA_ROPE,), F32), 1.0), place(s1, jnp.zeros((MLA_ROPE,), F32), 0.0),
            place(s2, jnp.zeros((MLA_ROPE,), F32), 0.0))


def _prep_w_in(w):
    o = {}
    off = 0
    for name, width in (("cq", MLA_Q_RANK), ("ckv", MLA_KV_RANK), ("krope", MLA_ROPE), ("sc", 3 * SC_WIDTH),
                        ("gdn", 2 * GDN_QK + 2 * GDN_VW), ("ab", 2 * N_DIR * GDN_HEADS)):
        o[name] = (off, width)
        off += width
    sl = lambda n: w[:, o[n][0]:o[n][0] + o[n][1]]
    small = jnp.concatenate([sl("cq"), sl("ckv"), sl("krope"), sl("ab")], axis=1).astype(BF16)
    return small, sl("sc").astype(BF16), sl("gdn").astype(BF16), w[:, off:].astype(BF16)


def _prep_w_uq(w):
    r = w.shape[0]
    w3 = w.reshape(r, MLA_HEADS, MLA_QK)
    w3 = jnp.concatenate([w3, jnp.zeros((r, MLA_HEADS, MLA_PAD - MLA_QK), w.dtype)], axis=-1)
    return w3.reshape(r, MLA_HEADS * MLA_PAD).astype(BF16)


def _prep_w_ukv(w):
    r = w.shape[0]
    w3 = w.reshape(r, MLA_HEADS, MLA_NOPE + MLA_V)
    k_top = jnp.concatenate([w3[..., :MLA_NOPE], jnp.zeros((r, MLA_HEADS, MLA_PAD - MLA_NOPE), w.dtype)], -1)
    v_top = w3[..., MLA_NOPE:]
    eye = jnp.eye(MLA_ROPE, dtype=w.dtype)[:, None, :]
    k_rope = jnp.concatenate([jnp.zeros((MLA_ROPE, MLA_HEADS, MLA_NOPE), w.dtype),
                              jnp.broadcast_to(eye, (MLA_ROPE, MLA_HEADS, MLA_ROPE)),
                              jnp.zeros((MLA_ROPE, MLA_HEADS, MLA_PAD - MLA_QK), w.dtype)], -1)
    pad_rows = KV_LHS_W - r - MLA_ROPE
    keys = jnp.concatenate([k_top.reshape(r, -1), k_rope.reshape(MLA_ROPE, -1),
                            jnp.zeros((pad_rows, MLA_HEADS * MLA_PAD), w.dtype)], 0)
    vals = jnp.concatenate([v_top.reshape(r, -1), jnp.zeros((MLA_ROPE + pad_rows, MLA_HEADS * MLA_V), w.dtype)], 0)
    return jnp.concatenate([keys, vals], axis=1).astype(BF16)


def kernel(x, c, ctx, c_ctx, w_ada, b_ada, w_in, q_norm, kv_norm, w_uq, w_ukv, sc_conv, gdn_conv, gdn_a_log,
           gdn_dt_bias, gdn_norm, w_branch, w_o, ln1_g, ln1_b, ln2_g, ln2_b, w_router, router_bias, w_e_gate,
           w_e_up, w_e_down):
    n_batch, t_lat, d = x.shape
    t_ctx = ctx.shape[1]
    depth = w_ada.shape[0]
    n_lat = n_batch * t_lat
    n_all = n_lat + n_batch * t_ctx
    alpha = (2 * depth) ** 0.25
    geo = dict(t_lat=t_lat, n_batch=n_batch)

    xa = jnp.concatenate([x.reshape(n_lat, d), ctx.reshape(n_batch * t_ctx, d)], axis=0)
    cc = jnp.zeros((SUBLANES, d), F32).at[:n_batch].set(c).at[n_batch].set(c_ctx)
    mods = [
        _matmul(cc, w_ada, b_layer=l, out_dtype=F32, tm=SUBLANES, tn=512, lhs_silu=True,
                extra=((b_ada[l][None, :], (1, 512), lambda i, j: (0, j)),), epilogue=_ep_bias,
                name="adaln").reshape(SUBLANES, ADA_MULT, d)
        for l in range(depth)
    ]
    q_tabs = _rope_tables(n_batch, t_lat, t_ctx, MLA_PAD, MLA_NOPE, MLA_QK ** -0.5)
    k_tabs = _rope_tables(n_batch, t_lat, t_ctx, LANES, 0, 1.0)
    w_router_t = w_router.T.astype(BF16)
    bias_col = router_bias.reshape(-1, 1)
    w_branch16, w_o16 = w_branch.astype(BF16), w_o.astype(BF16)
    lane_vec = lambda v: jnp.zeros((1, LANES), F32).at[0, 64:96].set(v.reshape(-1))

    h1 = _modulate(xa, mods[0], shift=0, scale=1, **geo)
    for l in range(depth):
        with_ctx = l < depth - 1
        m = n_all if with_ctx else n_lat
        w_small, w_sc, w_gdn, w_gates = _prep_w_in(w_in[l])

        wide = dict(tn=1024, cols_outer=True)
        p_small = _matmul(h1, w_small, out_dtype=F32, tn=SMALL_W, name="in_small")
        p_sc = _matmul(h1, w_sc, out_dtype=F32, m=m, name="in_sc", **wide)
        p_gdn = _matmul(h1, w_gdn, out_dtype=F32, name="in_gdn", **wide)
        gates = _matmul(h1, w_gates, out_dtype=BF16, m=m, epilogue=_ep_sigmoid, name="in_gates", **wide)

        cqn, kvl, gb = _small_prologue(p_small, q_norm[l][None, :], kv_norm[l][None, :], *k_tabs,
                                       lane_vec(-jnp.exp(gdn_a_log[l])), lane_vec(gdn_dt_bias[l]))
        q = _matmul(cqn, _prep_w_uq(w_uq[l]), out_dtype=BF16, m=m, tn=4 * MLA_PAD, epilogue=_ep_rope,
                    extra=tuple((t, (_tile(m, 512, SUBLANES), MLA_PAD), lambda i, j: (i, 0)) for t in q_tabs),
                    name="q_up")
        kv = _matmul(kvl, _prep_w_ukv(w_ukv[l]), out_dtype=BF16, tn=2048, name="kv_up")
        attn = _attention(q, kv, n_batch=n_batch, t_lat=t_lat, t_ctx=t_ctx, latent_queries=True)
        if with_ctx:
            attn_ctx = _attention(q, kv, n_batch=n_batch, t_lat=t_lat, t_ctx=t_ctx, latent_queries=False)
            attn = jnp.concatenate([attn, attn_ctx], axis=0)

        conv = _short_conv(p_sc, sc_conv[l], m=m, n_lat=n_lat, t_lat=t_lat, t_ctx=t_ctx)

        feat = _gdn_features(p_gdn, gdn_conv[l], n_lat=n_lat, t_lat=t_lat, t_ctx=t_ctx)
        o0, o1 = _gdn_scan(feat, gb, _pack_decay_rows(gb), n_batch=n_batch, t_lat=t_lat, t_ctx=t_ctx)
        gdn = _gdn_output(o0, o1, p_gdn, gdn_norm[l][None, :], m=m)

        merged = _merge(attn, conv, gdn, w_branch16, l, gates, m=m)
        mix = _matmul(merged, w_o16, b_layer=l, out_dtype=F32, name="w_o", **wide)
        x1, h2, logits = _ln_router(xa, mix, mods[l], ln1_g[l][None, :], ln1_b[l][None, :], w_router_t, m=m,
                                    alpha=alpha, **geo)

        idx, rank, wt, counts = _route(logits, bias_col)
        plan = _moe_plan(counts, m)
        dest = _dest(idx, rank, plan["gstart"])
        xs = _dispatch(h2, dest[0], dest[1], plan["pad0"], plan["pad1"], n_rows=plan["n_rows"])
        act = _expert_up(xs, w_e_gate, w_e_up, l, plan["tile_expert"], plan["used"])
        ys = _expert_down(act, w_e_down, l, plan["tile_expert"], plan["used"])
        res = _ln_combine(x1, ys, dest[0], dest[1], wt, mods[l], ln2_g[l][None, :], ln2_b[l][None, :], m=m,
                          alpha=alpha, mod_next=mods[l + 1] if with_ctx else None, **geo)
        xa = res[0]
        if with_ctx:
            h1 = res[1]
    return xa[:n_lat].reshape(n_batch, t_lat, d)
```
